```python
import jax, jax.numpy as jnp
from jax import lax
import numpy as np

D_MODEL = 1024
BATCH = 4
SEQ = 8192
DEPTH = 1

D_INNER = 2 * D_MODEL
M_WIDTH = D_INNER // 2
M_HEADS = 4
M_DV = M_WIDTH // M_HEADS
M_DQK = M_DV // 2
M_QK = M_HEADS * M_DQK
M_CHUNK = 128
GATE_CAP = 15.0
S_WIDTH = D_INNER - M_WIDTH
S_HEADDIM = 64
S_HEADS = S_WIDTH // S_HEADDIM
S_GROUPS = 2
S_STATE = 128
S_CONV = 4
S_CHUNK = 128
S_CONV_CH = S_WIDTH + 2 * S_GROUPS * S_STATE
EPS = 1e-6

IN_SIZES = (M_QK, M_QK, M_WIDTH, M_HEADS, M_HEADS, M_WIDTH, M_WIDTH, S_WIDTH, S_CONV_CH, S_HEADS)
D_IN_PROJ = sum(IN_SIZES)
SPLIT_POINTS = tuple(int(p) for p in np.cumsum(IN_SIZES)[:-1])

kernel_name = "hymba_mlstm_ssd_parallel_heads"


def rms_norm(x, w):
    xf = x.astype(jnp.float32)
    y = xf * lax.rsqrt(jnp.mean(xf * xf, axis=-1, keepdims=True) + EPS)
    return (y * w.astype(jnp.float32)).astype(x.dtype)


def soft_cap(u):
    return GATE_CAP * jnp.tanh(u / GATE_CAP)


def mlstm_chunkwise(q, k, v, log_i, log_f):
    f32 = jnp.float32
    Bsz, T = q.shape[0], q.shape[1]
    L = M_CHUNK
    NC = T // L
    q = q.astype(f32).reshape(Bsz, NC, L, M_HEADS, M_DQK) * (M_DQK ** -0.5)
    k = k.astype(f32).reshape(Bsz, NC, L, M_HEADS, M_DQK)
    v = v.astype(f32).reshape(Bsz, NC, L, M_HEADS, M_DV)
    li = log_i.reshape(Bsz, NC, L, M_HEADS).transpose(0, 3, 1, 2)
    lf = log_f.reshape(Bsz, NC, L, M_HEADS).transpose(0, 3, 1, 2)
    b = jnp.cumsum(lf, axis=-1)
    b_last = b[..., -1]
    g = b_last[..., None] - b + li
    g_max = jnp.max(g, axis=-1)
    w = jnp.exp(g - g_max[..., None])
    C_loc = jnp.einsum('bhcs,bcshk,bcshv->cbhkv', w, k, v)
    n_loc = jnp.einsum('bhcs,bcshk->cbhk', w, k)

    def step(carry, inp):
        C, n, m = carry
        a, gm, Cl, nl = inp
        m_new = jnp.maximum(a + m, gm)
        s_old = jnp.exp(a + m - m_new)
        s_loc = jnp.exp(gm - m_new)
        C_new = s_old[..., None, None] * C + s_loc[..., None, None] * Cl
        n_new = s_old[..., None] * n + s_loc[..., None] * nl
        return (C_new, n_new, m_new), (C, n, m)

    init = (jnp.zeros((Bsz, M_HEADS, M_DQK, M_DV), f32),
            jnp.zeros((Bsz, M_HEADS, M_DQK), f32),
            jnp.zeros((Bsz, M_HEADS), f32))
    xs = (jnp.moveaxis(b_last, 2, 0), jnp.moveaxis(g_max, 2, 0), C_loc, n_loc)
    _, (C_prev, n_prev, m_prev) = lax.scan(step, init, xs)
    m_prev = jnp.moveaxis(m_prev, 0, 2)

    causal = jnp.tril(jnp.ones((L, L), dtype=bool))
    D = b[..., :, None] - b[..., None, :] + li[..., None, :]
    D = jnp.where(causal, D, -jnp.inf)
    m_inter = b + m_prev[..., None]
    m_j = jnp.maximum(m_inter, jnp.max(D, axis=-1))
    S = jnp.einsum('bcjhk,bcshk->bhcjs', q, k) * jnp.exp(D - m_j[..., None])
    num = jnp.einsum('bhcjs,bcshv->bcjhv', S, v)
    den = jnp.sum(S, axis=-1)
    inter_scale = jnp.exp(m_inter - m_j)
    num = num + jnp.einsum('bcjhk,cbhkv->bcjhv', q, C_prev) * inter_scale.transpose(0, 2, 3, 1)[..., None]
    den = den + jnp.einsum('bcjhk,cbhk->bhcj', q, n_prev) * inter_scale
    denom = jnp.maximum(jnp.abs(den), jnp.exp(-m_j))
    h = num / denom.transpose(0, 2, 3, 1)[..., None]
    return h.reshape(Bsz, T, M_HEADS, M_DV)


def causal_depthwise_conv(u, w, b):
    kern = jnp.transpose(w)[:, None, :].astype(u.dtype)
    out = lax.conv_general_dilated(u, kern, window_strides=(1,), padding=[(S_CONV - 1, 0)],
                                   dimension_numbers=('NWC', 'WIO', 'NWC'),
                                   feature_group_count=u.shape[-1])
    return out + b.astype(u.dtype)


def ssd_chunked(x, dt, A, Bm, Cm):
    f32 = jnp.float32
    Bsz, T = x.shape[0], x.shape[1]
    L = S_CHUNK
    NC = T // L
    R = S_HEADS // S_GROUPS
    x = x.astype(f32).reshape(Bsz, NC, L, S_GROUPS, R, S_HEADDIM)
    dt = dt.reshape(Bsz, NC, L, S_GROUPS, R)
    Bm = Bm.astype(f32).reshape(Bsz, NC, L, S_GROUPS, S_STATE)
    Cm = Cm.astype(f32).reshape(Bsz, NC, L, S_GROUPS, S_STATE)
    a = (dt * A.reshape(S_GROUPS, R)).transpose(0, 3, 4, 1, 2)
    a_cum = jnp.cumsum(a, axis=-1)
    xdt = x * dt[..., None]
    causal = jnp.tril(jnp.ones((L, L), dtype=bool))
    decay = jnp.exp(jnp.where(causal, a_cum[..., :, None] - a_cum[..., None, :], -jnp.inf))
    CB = jnp.einsum('bclgn,bcsgn->bgcls', Cm, Bm)
    y_diag = jnp.einsum('bgcls,bgrcls,bcsgrp->bclgrp', CB, decay, xdt)
    decay_end = jnp.exp(a_cum[..., -1:] - a_cum)
    states = jnp.einsum('bcsgn,bgrcs,bcsgrp->cbgrpn', Bm, decay_end, xdt)
    chunk_decay = jnp.moveaxis(jnp.exp(a_cum[..., -1]), -1, 0)

    def step(h, inp):
        dA, s = inp
        return dA[..., None, None] * h + s, h

    h0 = jnp.zeros((Bsz, S_GROUPS, R, S_HEADDIM, S_STATE), f32)
    _, h_prev = lax.scan(step, h0, (chunk_decay, states))
    y_off = jnp.einsum('bclgn,cbgrpn,bgrcl->bclgrp', Cm, h_prev, jnp.exp(a_cum))
    return (y_diag + y_off).reshape(Bsz, T, S_HEADS, S_HEADDIM)


def setup_inputs(seed: int = 0) -> dict:
    key = jax.random.key(seed)
    ks = jax.random.split(key, 16)
    f32 = jnp.float32
    x = jax.random.normal(ks[0], (BATCH, SEQ, D_MODEL), f32)
    norm_w = 1.0 + 0.01 * jax.random.normal(ks[1], (DEPTH, D_MODEL), f32)
    w_in = jax.random.normal(ks[2], (DEPTH, D_MODEL, D_IN_PROJ), f32) * D_MODEL ** -0.5
    b_igate = 0.1 * jax.random.normal(ks[3], (DEPTH, M_HEADS), f32)
    b_fgate = jnp.linspace(3.0, 6.0, M_HEADS, dtype=f32)[None, :] + 0.1 * jax.random.normal(ks[4], (DEPTH, M_HEADS), f32)
    conv_w = jax.random.normal(ks[5], (DEPTH, S_CONV_CH, S_CONV), f32) * S_CONV ** -0.5
    conv_b = 0.01 * jax.random.normal(ks[6], (DEPTH, S_CONV_CH), f32)
    dt0 = jnp.exp(jax.random.uniform(ks[7], (DEPTH, S_HEADS), f32, minval=float(np.log(1e-3)), maxval=float(np.log(1e-1))))
    dt_bias = dt0 + jnp.log(-jnp.expm1(-dt0))
    a_log = jnp.log(jax.random.uniform(ks[8], (DEPTH, S_HEADS), f32, minval=1.0, maxval=16.0))
    d_skip = 1.0 + 0.1 * jax.random.normal(ks[9], (DEPTH, S_HEADS), f32)
    mlstm_norm_w = 1.0 + 0.01 * jax.random.normal(ks[10], (DEPTH, M_WIDTH), f32)
    ssd_norm_w = 1.0 + 0.01 * jax.random.normal(ks[11], (DEPTH, S_WIDTH), f32)
    w_out = jax.random.normal(ks[12], (DEPTH, D_INNER, D_MODEL), f32) * D_INNER ** -0.5
    final_norm_w = 1.0 + 0.01 * jax.random.normal(ks[13], (D_MODEL,), f32)
    return {"x": x, "norm_w": norm_w, "w_in": w_in, "b_igate": b_igate, "b_fgate": b_fgate,
            "conv_w": conv_w, "conv_b": conv_b, "dt_bias": dt_bias, "a_log": a_log,
            "d_skip": d_skip, "mlstm_norm_w": mlstm_norm_w, "ssd_norm_w": ssd_norm_w,
            "w_out": w_out, "final_norm_w": final_norm_w}


def reference(x, norm_w, w_in, b_igate, b_fgate, conv_w, conv_b, dt_bias, a_log, d_skip,
              mlstm_norm_w, ssd_norm_w, w_out, final_norm_w):
    f32 = jnp.float32
    dtype = x.dtype
    Bsz, T = x.shape[0], x.shape[1]
    for l in range(DEPTH):
        h = rms_norm(x, norm_w[l])
        proj = h @ w_in[l]
        q, k, v, i_pre, f_pre, o_pre, z_m, z_s, xbc, dt_raw = jnp.split(proj, SPLIT_POINTS, axis=-1)

        log_i = soft_cap(i_pre.astype(f32) + b_igate[l].astype(f32))
        log_f = jax.nn.log_sigmoid(soft_cap(f_pre.astype(f32) + b_fgate[l].astype(f32)))
        hm = mlstm_chunkwise(q.reshape(Bsz, T, M_HEADS, M_DQK),
                             k.reshape(Bsz, T, M_HEADS, M_DQK),
                             v.reshape(Bsz, T, M_HEADS, M_DV), log_i, log_f)
        hm = hm * lax.rsqrt(jnp.mean(hm * hm, axis=-1, keepdims=True) + EPS)
        hm = hm.reshape(Bsz, T, M_WIDTH) * mlstm_norm_w[l].astype(f32)
        hm = hm * jax.nn.sigmoid(o_pre.astype(f32)) * jax.nn.silu(z_m.astype(f32))

        xbc = jax.nn.silu(causal_depthwise_conv(xbc, conv_w[l], conv_b[l]))
        xs, Bm, Cm = jnp.split(xbc, (S_WIDTH, S_WIDTH + S_GROUPS * S_STATE), axis=-1)
        xs = xs.reshape(Bsz, T, S_HEADS, S_HEADDIM)
        dt = jax.nn.softplus(dt_raw.astype(f32) + dt_bias[l].astype(f32))
        A = -jnp.exp(a_log[l].astype(f32))
        y = ssd_chunked(xs, dt, A,
                        Bm.reshape(Bsz, T, S_GROUPS, S_STATE),
                        Cm.reshape(Bsz, T, S_GROUPS, S_STATE))
        y = y + d_skip[l].astype(f32)[:, None] * xs.astype(f32)
        y = y.reshape(Bsz, T, S_WIDTH) * jax.nn.silu(z_s.astype(f32))
        yg = y.reshape(Bsz, T, S_GROUPS, S_WIDTH // S_GROUPS)
        yg = yg * lax.rsqrt(jnp.mean(yg * yg, axis=-1, keepdims=True) + EPS)
        y = yg.reshape(Bsz, T, S_WIDTH) * ssd_norm_w[l].astype(f32)

        mix = jnp.concatenate([hm, y], axis=-1).astype(dtype)
        x = x + mix @ w_out[l]
    return rms_norm(x, final_norm_w)
```

```python
import functools

import jax
import jax.numpy as jnp
from jax import lax
from jax.experimental import pallas as pl
from jax.experimental.pallas import tpu as pltpu

F32 = jnp.float32
BF16 = jnp.bfloat16

D_MODEL = 1024
D_INNER = 2 * D_MODEL
M_WIDTH = D_INNER // 2
M_HEADS = 4
M_DV = M_WIDTH // M_HEADS
M_DQK = M_DV // 2
M_QK = M_HEADS * M_DQK
CHUNK = 128
GATE_CAP = 15.0
S_WIDTH = D_INNER - M_WIDTH
S_HEADDIM = 64
S_HEADS = S_WIDTH // S_HEADDIM
S_GROUPS = 2
S_HEADS_PER_GROUP = S_HEADS // S_GROUPS
S_GROUP_WIDTH = S_WIDTH // S_GROUPS
S_STATE = 128
S_CONV = 4
S_CONV_CH = S_WIDTH + 2 * S_GROUPS * S_STATE
EPS = 1e-6

LANES = 128
SUBLANES = 8
VMEM_LIMIT_BYTES = 56 * 1024 * 1024

OFF_Q = 0
OFF_K = OFF_Q + M_QK
OFF_V = OFF_K + M_QK
OFF_O = OFF_V + M_WIDTH
OFF_ZM = OFF_O + M_WIDTH
OFF_ZS = OFF_ZM + M_WIDTH
OFF_XBC = OFF_ZS + S_WIDTH
N_MAIN = OFF_XBC + S_CONV_CH
SM_I = 0
SM_F = SM_I + M_HEADS
SM_DT = SM_F + M_HEADS
SM_END = SM_DT + S_HEADS

TM_IN = 512
TM_OUT = 512


def _sigmoid(u):
    return 1.0 / (1.0 + jnp.exp(-u))


def _silu(u):
    return u * _sigmoid(u)


def _softplus(u):
    return jnp.maximum(u, 0.0) + jnp.log1p(jnp.exp(-jnp.abs(u)))


def _in_proj_kernel(x_ref, nw_ref, w_ref, wg_ref, gb_ref, cw_ref, cb_ref,
                    q_ref, k_ref, v_ref, og_ref, zs_ref, xs_ref, bm_ref, cm_ref, small_ref,
                    xn_ref, cbuf_ref, *, tiles_per_seq):
    tm = x_ref.shape[0]
    x = x_ref[...]
    ms = jnp.mean(x * x, axis=-1, keepdims=True)
    xn_ref[...] = (x * lax.rsqrt(ms + EPS) * nw_ref[...]).astype(BF16)

    def proj(lo, hi):
        return jnp.dot(xn_ref[...], w_ref[:, lo:hi], preferred_element_type=F32)

    q_ref[...] = (proj(OFF_Q, OFF_K) * (M_DQK ** -0.5)).astype(BF16)
    k_ref[...] = proj(OFF_K, OFF_V).astype(BF16)
    v_ref[...] = proj(OFF_V, OFF_O).astype(BF16)
    og_ref[...] = (_sigmoid(proj(OFF_O, OFF_ZM)) * _silu(proj(OFF_ZM, OFF_ZS))).astype(BF16)
    zs_ref[...] = _silu(proj(OFF_ZS, OFF_XBC)).astype(BF16)

    @pl.when(pl.program_id(0) % tiles_per_seq == 0)
    def _():
        cbuf_ref[0:SUBLANES, :] = jnp.zeros((SUBLANES, S_CONV_CH), F32)

    cbuf_ref[SUBLANES:SUBLANES + tm, :] = proj(OFF_XBC, N_MAIN)
    acc = cb_ref[...] + cw_ref[S_CONV - 1:S_CONV, :] * cbuf_ref[SUBLANES:SUBLANES + tm, :]
    for j in range(1, S_CONV):
        acc = acc + (cw_ref[S_CONV - 1 - j:S_CONV - j, :]
                     * cbuf_ref[SUBLANES - j:SUBLANES - j + tm, :])
    tail = cbuf_ref[tm:tm + SUBLANES, :]
    cbuf_ref[0:SUBLANES, :] = tail
    xbc = _silu(acc)
    xs_ref[...] = xbc[:, 0:S_WIDTH].astype(BF16)
    bm_ref[...] = xbc[:, S_WIDTH:S_WIDTH + S_GROUPS * S_STATE].astype(BF16)
    cm_ref[...] = xbc[:, S_WIDTH + S_GROUPS * S_STATE:S_CONV_CH].astype(BF16)

    g = jnp.dot(xn_ref[...], wg_ref[...], preferred_element_type=F32) + gb_ref[...]
    col = lax.broadcasted_iota(jnp.int32, g.shape, 1)
    capped = GATE_CAP * jnp.tanh(g / GATE_CAP)
    log_f = jnp.minimum(capped, 0.0) - jnp.log1p(jnp.exp(-jnp.abs(capped)))
    small = jnp.where(col < SM_F, capped,
                      jnp.where(col < SM_DT, log_f,
                                jnp.where(col < SM_END, _softplus(g), 0.0)))
    small_ref[...] = small


def _in_proj(x2, nw, w_main, w_gate, gate_bias, conv_w_t, conv_b, *, seq):
    n_tok = x2.shape[0]
    tm = TM_IN
    grid = (n_tok // tm,)
    row = lambda i: (i, 0)
    fixed = lambda i: (0, 0)
    out_shapes = (
        jax.ShapeDtypeStruct((n_tok, M_QK), BF16),
        jax.ShapeDtypeStruct((n_tok, M_QK), BF16),
        jax.ShapeDtypeStruct((n_tok, M_WIDTH), BF16),
        jax.ShapeDtypeStruct((n_tok, M_WIDTH), BF16),
        jax.ShapeDtypeStruct((n_tok, S_WIDTH), BF16),
        jax.ShapeDtypeStruct((n_tok, S_WIDTH), BF16),
        jax.ShapeDtypeStruct((n_tok, S_GROUPS * S_STATE), BF16),
        jax.ShapeDtypeStruct((n_tok, S_GROUPS * S_STATE), BF16),
        jax.ShapeDtypeStruct((n_tok, LANES), F32),
    )
    out_specs = tuple(pl.BlockSpec((tm, s.shape[1]), row) for s in out_shapes)
    return pl.pallas_call(
        functools.partial(_in_proj_kernel, tiles_per_seq=seq // tm),
        grid=grid,
        in_specs=[
            pl.BlockSpec((tm, D_MODEL), row),
            pl.BlockSpec((1, D_MODEL), fixed),
            pl.BlockSpec((D_MODEL, N_MAIN), fixed),
            pl.BlockSpec((D_MODEL, LANES), fixed),
            pl.BlockSpec((1, LANES), fixed),
            pl.BlockSpec((S_CONV, S_CONV_CH), fixed),
            pl.BlockSpec((1, S_CONV_CH), fixed),
        ],
        out_specs=out_specs,
        out_shape=out_shapes,
        scratch_shapes=[
            pltpu.VMEM((tm, D_MODEL), BF16),
            pltpu.VMEM((tm + SUBLANES, S_CONV_CH), F32),
        ],
        compiler_params=pltpu.CompilerParams(
            dimension_semantics=("arbitrary",),
            vmem_limit_bytes=VMEM_LIMIT_BYTES),
        name="in_proj",
    )(x2, nw, w_main, w_gate, gate_bias, conv_w_t, conv_b)


def _tri_masks():
    r = lax.broadcasted_iota(jnp.int32, (CHUNK, CHUNK), 0)
    c = lax.broadcasted_iota(jnp.int32, (CHUNK, CHUNK), 1)
    return c <= r


def _cumsum_cols(causal, vals):
    tril = jnp.where(causal, 1.0, 0.0).astype(F32)
    return jnp.dot(tril, vals, preferred_element_type=F32, precision=lax.Precision.HIGHEST)


def _cumsum_rows(causal, vals_t):
    triu = jnp.where(causal, 1.0, 0.0).astype(F32).T
    return jnp.dot(vals_t, triu, preferred_element_type=F32, precision=lax.Precision.HIGHEST)


def _mlstm_kernel(q_ref, k_ref, v_ref, og_ref, small_ref, nw_ref, o_ref, c_ref, n_ref, m_ref):
    @pl.when(pl.program_id(1) == 0)
    def _():
        c_ref[...] = jnp.zeros(c_ref.shape, F32)
        n_ref[...] = jnp.zeros(n_ref.shape, F32)
        m_ref[...] = jnp.zeros(m_ref.shape, F32)

    causal = _tri_masks()
    small = small_ref[...]
    small_t = small.T
    cum_cols = _cumsum_cols(causal, small)
    cum_rows = _cumsum_rows(causal, small_t[0:SUBLANES, :])

    for h in range(M_HEADS):
        q = q_ref[:, h * M_DQK:(h + 1) * M_DQK]
        k = k_ref[:, h * M_DQK:(h + 1) * M_DQK]
        v = v_ref[:, h * M_DV:(h + 1) * M_DV]
        li_col = small[:, SM_I + h:SM_I + h + 1]
        b_col = cum_cols[:, SM_F + h:SM_F + h + 1]
        li_row = small_t[SM_I + h:SM_I + h + 1, :]
        b_row = cum_rows[SM_F + h:SM_F + h + 1, :]
        c_prev = c_ref[h]
        n_prev = n_ref[h]
        m_prev = m_ref[h][:, 0:1]

        d = jnp.where(causal, b_col + (li_row - b_row), -jnp.inf)
        m_inter = b_col + m_prev
        m_j = jnp.maximum(m_inter, jnp.max(d, axis=-1, keepdims=True))
        qk = lax.dot_general(q, k, (((1,), (1,)), ((), ())), preferred_element_type=F32)
        s = qk * jnp.exp(d - m_j)
        inter_scale = jnp.exp(m_inter - m_j)
        qf = q.astype(F32)
        num = (jnp.dot(s.astype(BF16), v, preferred_element_type=F32)
               + jnp.dot(q, c_prev.astype(BF16), preferred_element_type=F32) * inter_scale)
        den = (jnp.sum(s, axis=-1, keepdims=True)
               + jnp.sum(qf * n_prev, axis=-1, keepdims=True) * inter_scale)
        denom = jnp.maximum(jnp.abs(den), jnp.exp(-m_j))
        hval = num * (1.0 / denom)
        hval = hval * lax.rsqrt(jnp.mean(hval * hval, axis=-1, keepdims=True) + EPS)
        hval = hval * nw_ref[:, h * M_DV:(h + 1) * M_DV]
        hval = hval * og_ref[:, h * M_DV:(h + 1) * M_DV].astype(F32)
        o_ref[:, h * M_DV:(h + 1) * M_DV] = hval.astype(o_ref.dtype)

        b_last = b_col[CHUNK - 1:CHUNK, :]
        g = b_last - b_col + li_col
        g_max = jnp.max(g, axis=0, keepdims=True)
        kw = k.astype(F32) * jnp.exp(g - g_max)
        c_loc = lax.dot_general(kw.astype(BF16), v, (((0,), (0,)), ((), ())),
                                preferred_element_type=F32)
        n_loc = jnp.sum(kw, axis=0, keepdims=True)
        m_new = jnp.maximum(b_last + m_prev, g_max)
        s_old = jnp.exp(b_last + m_prev - m_new)
        s_loc = jnp.exp(g_max - m_new)
        c_ref[h] = s_old * c_prev + s_loc * c_loc
        n_ref[h] = s_old * n_prev + s_loc * n_loc
        m_ref[h] = jnp.broadcast_to(m_new, (1, LANES))


def _mlstm(q, k, v, og, small, nw):
    bsz, seq = q.shape[0], q.shape[1]
    grid = (bsz, seq // CHUNK)
    blk = lambda w: pl.BlockSpec((None, CHUNK, w), lambda b, c: (b, c, 0))
    return pl.pallas_call(
        _mlstm_kernel,
        grid=grid,
        in_specs=[blk(M_QK), blk(M_QK), blk(M_WIDTH), blk(M_WIDTH), blk(LANES),
                  pl.BlockSpec((1, M_WIDTH), lambda b, c: (0, 0))],
        out_specs=blk(M_WIDTH),
        out_shape=jax.ShapeDtypeStruct((bsz, seq, M_WIDTH), BF16),
        scratch_shapes=[
            pltpu.VMEM((M_HEADS, M_DQK, M_DV), F32),
            pltpu.VMEM((M_HEADS, 1, M_DQK), F32),
            pltpu.VMEM((M_HEADS, 1, LANES), F32),
        ],
        compiler_params=pltpu.CompilerParams(
            dimension_semantics=("arbitrary", "arbitrary"),
            vmem_limit_bytes=VMEM_LIMIT_BYTES),
        name="mlstm",
    )(q, k, v, og, small, nw)


def _ssd_kernel(xs_ref, bm_ref, cm_ref, zs_ref, small_ref, alog_ref, dskip_ref, nw_ref,
                o_ref, h_ref):
    @pl.when(pl.program_id(1) == 0)
    def _():
        h_ref[...] = jnp.zeros(h_ref.shape, F32)

    causal = _tri_masks()
    lane = lax.broadcasted_iota(jnp.int32, (CHUNK, LANES), 1)
    first_half = lane < S_HEADDIM
    small = small_ref[...]
    col = lax.broadcasted_iota(jnp.int32, (1, LANES), 1)
    a_neg = jnp.where((col >= SM_DT) & (col < SM_END), -jnp.exp(alog_ref[...]), 0.0)
    a_all = small * a_neg
    acum_cols = _cumsum_cols(causal, a_all)
    acum_rows = _cumsum_rows(causal, a_all.T[SM_DT:SM_END, :])

    for g in range(S_GROUPS):
        bm = bm_ref[:, g * S_STATE:(g + 1) * S_STATE]
        cm = cm_ref[:, g * S_STATE:(g + 1) * S_STATE]
        cb = lax.dot_general(cm, bm, (((1,), (1,)), ((), ())), preferred_element_type=F32)
        h_prev = h_ref[g]
        y_off = jnp.dot(cm, h_prev.astype(BF16), preferred_element_type=F32)

        ys, xds, eas = [], [], []
        for p in range(S_HEADS_PER_GROUP // 2):
            i0 = g * S_HEADS_PER_GROUP + 2 * p
            lo = i0 * S_HEADDIM
            xs = xs_ref[:, lo:lo + LANES].astype(F32)
            dt_x = jnp.where(first_half, small[:, SM_DT + i0:SM_DT + i0 + 1],
                             small[:, SM_DT + i0 + 1:SM_DT + i0 + 2])
            ac_x = jnp.where(first_half, acum_cols[:, SM_DT + i0:SM_DT + i0 + 1],
                             acum_cols[:, SM_DT + i0 + 1:SM_DT + i0 + 2])
            xdt = xs * dt_x
            ms = []
            for i in (i0, i0 + 1):
                seg = (acum_cols[:, SM_DT + i:SM_DT + i + 1] - acum_rows[i:i + 1, :])
                decay = jnp.exp(jnp.where(causal, seg, -jnp.inf))
                ms.append((cb * decay).astype(BF16))
            m_pair = jnp.concatenate(ms, axis=1)
            rhs = jnp.concatenate([jnp.where(first_half, xdt, 0.0),
                                   jnp.where(first_half, 0.0, xdt)], axis=0).astype(BF16)
            ys.append(jnp.dot(m_pair, rhs, preferred_element_type=F32)
                      + dskip_ref[:, lo:lo + LANES] * xs)
            a_last = ac_x[CHUNK - 1:CHUNK, :]
            xds.append((xdt * jnp.exp(a_last - ac_x)).astype(BF16))
            eas.append(jnp.exp(ac_x))

        ea = jnp.concatenate(eas, axis=1)
        y = jnp.concatenate(ys, axis=1) + y_off * ea
        xd = jnp.concatenate(xds, axis=1)
        h_ref[g] = (ea[CHUNK - 1:CHUNK, :] * h_prev
                    + lax.dot_general(bm, xd, (((0,), (0,)), ((), ())),
                                      preferred_element_type=F32))

        glo = g * S_GROUP_WIDTH
        y = y * zs_ref[:, glo:glo + S_GROUP_WIDTH].astype(F32)
        y = y * lax.rsqrt(jnp.mean(y * y, axis=-1, keepdims=True) + EPS)
        y = y * nw_ref[:, glo:glo + S_GROUP_WIDTH]
        o_ref[:, glo:glo + S_GROUP_WIDTH] = y.astype(o_ref.dtype)


def _ssd(xs, bm, cm, zs, small, alog_row, dskip_row, nw):
    bsz, seq = xs.shape[0], xs.shape[1]
    grid = (bsz, seq // CHUNK)
    blk = lambda w: pl.BlockSpec((None, CHUNK, w), lambda b, c: (b, c, 0))
    fixed = lambda w: pl.BlockSpec((1, w), lambda b, c: (0, 0))
    return pl.pallas_call(
        _ssd_kernel,
        grid=grid,
        in_specs=[blk(S_WIDTH), blk(S_GROUPS * S_STATE), blk(S_GROUPS * S_STATE), blk(S_WIDTH),
                  blk(LANES), fixed(LANES), fixed(S_WIDTH), fixed(S_WIDTH)],
        out_specs=blk(S_WIDTH),
        out_shape=jax.ShapeDtypeStruct((bsz, seq, S_WIDTH), BF16),
        scratch_shapes=[pltpu.VMEM((S_GROUPS, S_STATE, S_GROUP_WIDTH), F32)],
        compiler_params=pltpu.CompilerParams(
            dimension_semantics=("arbitrary", "arbitrary"),
            vmem_limit_bytes=VMEM_LIMIT_BYTES),
        name="ssd",
    )(xs, bm, cm, zs, small, alog_row, dskip_row, nw)


def _out_proj_kernel(x_ref, hm_ref, y_ref, wm_ref, ws_ref, nw_ref, o_ref):
    acc = x_ref[...] + jnp.dot(hm_ref[...], wm_ref[...], preferred_element_type=F32)
    acc = acc + jnp.dot(y_ref[...], ws_ref[...], preferred_element_type=F32)
    ms = jnp.mean(acc * acc, axis=-1, keepdims=True)
    o_ref[...] = acc * lax.rsqrt(ms + EPS) * nw_ref[...]


def _out_proj(x2, hm, y, w_m, w_s, nw):
    n_tok = x2.shape[0]
    tm = TM_OUT
    row = lambda i: (i, 0)
    fixed = lambda i: (0, 0)
    return pl.pallas_call(
        _out_proj_kernel,
        grid=(n_tok // tm,),
        in_specs=[
            pl.BlockSpec((tm, D_MODEL), row),
            pl.BlockSpec((tm, M_WIDTH), row),
            pl.BlockSpec((tm, S_WIDTH), row),
            pl.BlockSpec((M_WIDTH, D_MODEL), fixed),
            pl.BlockSpec((S_WIDTH, D_MODEL), fixed),
            pl.BlockSpec((1, D_MODEL), fixed),
        ],
        out_specs=pl.BlockSpec((tm, D_MODEL), row),
        out_shape=jax.ShapeDtypeStruct((n_tok, D_MODEL), F32),
        compiler_params=pltpu.CompilerParams(
            dimension_semantics=("arbitrary",),
            vmem_limit_bytes=VMEM_LIMIT_BYTES),
        name="out_proj",
    )(x2, hm, y, w_m, w_s, nw)


def _pad_row(parts, width):
    row = jnp.concatenate([p.astype(F32).reshape(-1) for p in parts])
    return jnp.pad(row, (0, width - row.shape[0])).reshape(1, width)


def _layer(x, norm_w, w_in, b_igate, b_fgate, conv_w, conv_b, dt_bias, a_log, d_skip,
           mlstm_norm_w, ssd_norm_w, w_out, out_norm_w):
    bsz, seq, _ = x.shape
    assert seq % max(TM_IN, TM_OUT, CHUNK) == 0
    n_tok = bsz * seq
    x2 = x.reshape(n_tok, D_MODEL)

    edges = [0]
    for w in (M_QK, M_QK, M_WIDTH, M_HEADS, M_HEADS, M_WIDTH, M_WIDTH, S_WIDTH, S_CONV_CH, S_HEADS):
        edges.append(edges[-1] + w)
    seg = lambda j: w_in[:, edges[j]:edges[j + 1]]
    w_main = jnp.concatenate([seg(0), seg(1), seg(2), seg(5), seg(6), seg(7), seg(8)],
                             axis=1).astype(BF16)
    w_gate = jnp.concatenate([seg(3), seg(4), seg(9)], axis=1)
    w_gate = jnp.pad(w_gate, ((0, 0), (0, LANES - w_gate.shape[1]))).astype(BF16)
    gate_bias = _pad_row([b_igate, b_fgate, dt_bias], LANES)

    q, k, v, og, zs, xs, bm, cm, small = _in_proj(
        x2, norm_w.reshape(1, D_MODEL), w_main, w_gate, gate_bias,
        conv_w.T.astype(F32), conv_b.reshape(1, S_CONV_CH).astype(F32), seq=seq)

    r3 = lambda a: a.reshape(bsz, seq, a.shape[-1])
    small3 = r3(small)
    hm = _mlstm(r3(q), r3(k), r3(v), r3(og), small3, mlstm_norm_w.reshape(1, M_WIDTH))
    alog_row = _pad_row([jnp.zeros((SM_DT,), F32), a_log], LANES)
    dskip_row = jnp.repeat(d_skip.astype(F32), S_HEADDIM).reshape(1, S_WIDTH)
    y = _ssd(r3(xs), r3(bm), r3(cm), r3(zs), small3, alog_row, dskip_row,
             ssd_norm_w.reshape(1, S_WIDTH))

    w_o = w_out.astype(BF16)
    out = _out_proj(x2, hm.reshape(n_tok, M_WIDTH), y.reshape(n_tok, S_WIDTH),
                    w_o[:M_WIDTH], w_o[M_WIDTH:], out_norm_w.reshape(1, D_MODEL))
    return out.reshape(bsz, seq, D_MODEL)


def kernel(x, norm_w, w_in, b_igate, b_fgate, conv_w, conv_b, dt_bias, a_log, d_skip,
           mlstm_norm_w, ssd_norm_w, w_out, final_norm_w):
    depth = norm_w.shape[0]
    assert depth == 1, "final RMSNorm is fused into the single layer's output projection"
    return _layer(x, norm_w[0], w_in[0], b_igate[0], b_fgate[0], conv_w[0], conv_b[0],
                  dt_bias[0], a_log[0], d_skip[0], mlstm_norm_w[0], ssd_norm_w[0], w_out[0],
                  final_norm_w)
```

```python
import functools

import jax
import jax.numpy as jnp
from jax import lax
from jax.experimental import pallas as pl
from jax.experimental.pallas import tpu as pltpu

F32 = jnp.float32
BF16 = jnp.bfloat16

D_MODEL = 1024
D_INNER = 2 * D_MODEL
M_WIDTH = D_INNER // 2
M_HEADS = 4
M_DV = M_WIDTH // M_HEADS
M_DQK = M_DV // 2
M_QK = M_HEADS * M_DQK
CHUNK = 128
GATE_CAP = 15.0
S_WIDTH = D_INNER - M_WIDTH
S_HEADDIM = 64
S_HEADS = S_WIDTH // S_HEADDIM
S_GROUPS = 2
S_HEADS_PER_GROUP = S_HEADS // S_GROUPS
S_GROUP_WIDTH = S_WIDTH // S_GROUPS
S_STATE = 128
S_CONV = 4
S_CONV_CH = S_WIDTH + 2 * S_GROUPS * S_STATE
EPS = 1e-6

LANES = 128
SUBLANES = 8
BF16_ROWS = 16
VMEM_LIMIT_BYTES = 56 * 1024 * 1024

OFF_Q = 0
OFF_K = OFF_Q + M_QK
OFF_XBC = OFF_K + M_QK
N_ROWMAJOR = OFF_XBC + S_CONV_CH
OFF_V = 0
OFF_O = OFF_V + M_WIDTH
OFF_ZM = OFF_O + M_WIDTH
OFF_ZS = OFF_ZM + M_WIDTH
N_TRANSPOSED = OFF_ZS + S_WIDTH
G_I = 0
G_F = G_I + M_HEADS
G_DT = G_F + M_HEADS
G_END = G_DT + S_HEADS
G_ROWS = 32

TM_IN = 512
TM_OUT = 512
CHUNKS_PER_STEP = 2

NT = (((1,), (1,)), ((), ()))
TN = (((0,), (0,)), ((), ()))


def _sigmoid(u):
    return 1.0 / (1.0 + jnp.exp(-u))


def _silu(u):
    return u * _sigmoid(u)


def _softplus(u):
    return jnp.maximum(u, 0.0) + jnp.log1p(jnp.exp(-jnp.abs(u)))


def _in_proj_kernel(x_ref, nw_ref, wn_ref, wt_ref, wg_ref, gb_ref, cw_ref, cb_ref,
                    q_ref, k_ref, vt_ref, ogt_ref, zst_ref, xst_ref, bm_ref, cm_ref, gt_ref,
                    xn_ref, cbuf_ref, *, tiles_per_seq):
    tm = x_ref.shape[0]
    n_chunks = tm // CHUNK
    x = x_ref[...]
    ms = jnp.mean(x * x, axis=-1, keepdims=True)
    xn_ref[...] = (x * lax.rsqrt(ms + EPS) * nw_ref[...]).astype(BF16)

    def proj(lo, hi):
        return jnp.dot(xn_ref[...], wn_ref[:, lo:hi], preferred_element_type=F32)

    def proj_t(lo, hi):
        return lax.dot_general(wt_ref[lo:hi, :], xn_ref[...], NT, preferred_element_type=F32)

    def store_chunks(ref, val):
        for c in range(n_chunks):
            ref[c] = val[:, c * CHUNK:(c + 1) * CHUNK].astype(ref.dtype)

    q_ref[...] = (proj(OFF_Q, OFF_K) * (M_DQK ** -0.5)).astype(BF16)
    k_ref[...] = proj(OFF_K, OFF_XBC).astype(BF16)
    store_chunks(vt_ref, proj_t(OFF_V, OFF_O))
    store_chunks(ogt_ref, _sigmoid(proj_t(OFF_O, OFF_ZM)) * _silu(proj_t(OFF_ZM, OFF_ZS)))
    store_chunks(zst_ref, _silu(proj_t(OFF_ZS, N_TRANSPOSED)))

    @pl.when(pl.program_id(0) % tiles_per_seq == 0)
    def _():
        cbuf_ref[0:SUBLANES, :] = jnp.zeros((SUBLANES, S_CONV_CH), F32)

    cbuf_ref[SUBLANES:SUBLANES + tm, :] = proj(OFF_XBC, N_ROWMAJOR)
    acc = cb_ref[...] + cw_ref[S_CONV - 1:S_CONV, :] * cbuf_ref[SUBLANES:SUBLANES + tm, :]
    for j in range(1, S_CONV):
        acc = acc + (cw_ref[S_CONV - 1 - j:S_CONV - j, :]
                     * cbuf_ref[SUBLANES - j:SUBLANES - j + tm, :])
    tail = cbuf_ref[tm:tm + SUBLANES, :]
    cbuf_ref[0:SUBLANES, :] = tail
    xbc = _silu(acc)
    store_chunks(xst_ref, xbc[:, 0:S_WIDTH].T)
    bm_ref[...] = xbc[:, S_WIDTH:S_WIDTH + S_GROUPS * S_STATE].astype(BF16)
    cm_ref[...] = xbc[:, S_WIDTH + S_GROUPS * S_STATE:S_CONV_CH].astype(BF16)

    g = lax.dot_general(wg_ref[...], xn_ref[...], NT, preferred_element_type=F32) + gb_ref[...]
    row = lax.broadcasted_iota(jnp.int32, g.shape, 0)
    capped = GATE_CAP * jnp.tanh(g / GATE_CAP)
    log_f = jnp.minimum(capped, 0.0) - jnp.log1p(jnp.exp(-jnp.abs(capped)))
    gates = jnp.where(row < G_F, capped,
                      jnp.where(row < G_DT, log_f,
                                jnp.where(row < G_END, _softplus(g), 0.0)))
    store_chunks(gt_ref, gates)


def _in_proj(x2, nw, w_n, w_t, w_g, gate_bias, conv_w_t, conv_b, *, bsz, seq):
    tm = TM_IN
    tiles_per_seq = seq // tm
    n_chunks = seq // CHUNK
    cpt = tm // CHUNK
    grid = (bsz * tiles_per_seq,)
    row = lambda i: (i, 0)
    fixed = lambda i: (0, 0)
    chunked = lambda i: (i // tiles_per_seq, i % tiles_per_seq, 0, 0)

    def rowmajor(width):
        return (jax.ShapeDtypeStruct((bsz * seq, width), BF16), pl.BlockSpec((tm, width), row))

    def transposed(channels, dtype):
        return (jax.ShapeDtypeStruct((bsz, n_chunks, channels, CHUNK), dtype),
                pl.BlockSpec((None, cpt, channels, CHUNK), chunked))

    outs = [rowmajor(M_QK), rowmajor(M_QK),
            transposed(M_WIDTH, BF16), transposed(M_WIDTH, BF16),
            transposed(S_WIDTH, BF16), transposed(S_WIDTH, BF16),
            rowmajor(S_GROUPS * S_STATE), rowmajor(S_GROUPS * S_STATE),
            transposed(G_ROWS, F32)]
    return pl.pallas_call(
        functools.partial(_in_proj_kernel, tiles_per_seq=tiles_per_seq),
        grid=grid,
        in_specs=[
            pl.BlockSpec((tm, D_MODEL), row),
            pl.BlockSpec((1, D_MODEL), fixed),
            pl.BlockSpec((D_MODEL, N_ROWMAJOR), fixed),
            pl.BlockSpec((N_TRANSPOSED, D_MODEL), fixed),
            pl.BlockSpec((G_ROWS, D_MODEL), fixed),
            pl.BlockSpec((G_ROWS, 1), fixed),
            pl.BlockSpec((S_CONV, S_CONV_CH), fixed),
            pl.BlockSpec((1, S_CONV_CH), fixed),
        ],
        out_specs=tuple(o[1] for o in outs),
        out_shape=tuple(o[0] for o in outs),
        scratch_shapes=[
            pltpu.VMEM((tm, D_MODEL), BF16),
            pltpu.VMEM((tm + SUBLANES, S_CONV_CH), F32),
        ],
        compiler_params=pltpu.CompilerParams(
            dimension_semantics=("arbitrary",),
            vmem_limit_bytes=VMEM_LIMIT_BYTES),
        name="in_proj",
    )(x2, nw, w_n, w_t, w_g, gate_bias, conv_w_t, conv_b)


def _cumsum_lanes(rows, upper):
    return jnp.dot(rows, upper, preferred_element_type=F32, precision=lax.Precision.HIGHEST)


def _column_of(row):
    return jnp.broadcast_to(row, (CHUNK, CHUNK)).T


def _mlstm_chunk(c, rows, gates, upper, causal, q_ref, k_ref, vt_ref, ogt_ref, nw_ref,
                 o_ref, ct_ref, m_ref):
    cum = _cumsum_lanes(gates[0:SUBLANES, :], upper)
    b_all = cum[G_F:G_F + M_HEADS, :]
    u_all = gates[G_I:G_I + M_HEADS, :] - b_all
    ones = jnp.ones((BF16_ROWS, CHUNK), BF16)

    for h in range(M_HEADS):
        q = q_ref[rows, h * M_DQK:(h + 1) * M_DQK]
        k = k_ref[rows, h * M_DQK:(h + 1) * M_DQK]
        ch = slice(h * M_DV, (h + 1) * M_DV)
        vt_aug = jnp.concatenate([vt_ref[c, ch, :], ones], axis=0)
        b = b_all[h:h + 1, :]
        u_col = _column_of(u_all[h:h + 1, :])
        u_msk = jnp.where(causal, u_col, -jnp.inf)
        m_prev = m_ref[h][:, 0:1]
        ct_prev = ct_ref[h]

        mm = jnp.maximum(m_prev, jnp.max(u_msk, axis=0, keepdims=True))
        st = lax.dot_general(k, q, NT, preferred_element_type=F32) * jnp.exp(u_msk - mm)
        inter_scale = jnp.exp(m_prev - mm)
        num_aug = (jnp.dot(vt_aug, st.astype(BF16), preferred_element_type=F32)
                   + lax.dot_general(ct_prev.astype(BF16), q, NT, preferred_element_type=F32)
                   * inter_scale)
        num = num_aug[0:M_DV, :]
        den = num_aug[M_DV:M_DV + 1, :]
        rden = 1.0 / jnp.maximum(jnp.abs(den), jnp.exp(-(b + mm)))
        msq = jnp.mean(num * num, axis=0, keepdims=True)
        scale = rden * lax.rsqrt(rden * rden * msq + EPS)
        out = num * scale * nw_ref[ch, :] * ogt_ref[c, ch, :].astype(F32)
        o_ref[c, ch, :] = out.astype(o_ref.dtype)

        mm_last = mm[:, CHUNK - 1:CHUNK]
        kw = (k.astype(F32) * jnp.exp(u_col - mm_last)).astype(BF16)
        ct_ref[h] = (inter_scale[:, CHUNK - 1:CHUNK] * ct_prev
                     + jnp.dot(vt_aug, kw, preferred_element_type=F32))
        m_ref[h] = jnp.broadcast_to(b[:, CHUNK - 1:CHUNK] + mm_last, (1, LANES))


def _ssd_chunk(c, rows, gates, upper, causal, xst_ref, bm_ref, cm_ref, zst_ref, alog_ref,
               dskip_ref, nw_ref, o_ref, h_ref):
    dt = gates[G_DT:G_END, :]
    acum = _cumsum_lanes(dt * (-jnp.exp(alog_ref[...])), upper)
    ea = jnp.exp(acum)
    a_last = acum[:, CHUNK - 1:CHUNK]
    decay_end = jnp.exp(a_last - acum)
    chunk_decay = jnp.exp(a_last)

    for g in range(S_GROUPS):
        bm = bm_ref[rows, g * S_STATE:(g + 1) * S_STATE]
        cm = cm_ref[rows, g * S_STATE:(g + 1) * S_STATE]
        cbt = lax.dot_general(bm, cm, NT, preferred_element_type=F32)
        h_prev = h_ref[g]
        y_off = lax.dot_general(h_prev.astype(BF16), cm, NT, preferred_element_type=F32)
        ys, xds = [], []
        for r in range(S_HEADS_PER_GROUP):
            i = g * S_HEADS_PER_GROUP + r
            ch = slice(i * S_HEADDIM, (i + 1) * S_HEADDIM)
            a_row = acum[i:i + 1, :]
            decay = jnp.exp(jnp.where(causal, a_row - _column_of(a_row), -jnp.inf))
            mt = (cbt * decay).astype(BF16)
            xs = xst_ref[c, ch, :].astype(F32)
            xdt = xs * dt[i:i + 1, :]
            y = (jnp.dot(xdt.astype(BF16), mt, preferred_element_type=F32)
                 + y_off[r * S_HEADDIM:(r + 1) * S_HEADDIM, :] * ea[i:i + 1, :]
                 + dskip_ref[ch, :] * xs)
            ys.append(y * zst_ref[c, ch, :].astype(F32))
            xds.append((xdt * decay_end[i:i + 1, :]).astype(BF16))
        gch = slice(g * S_GROUP_WIDTH, (g + 1) * S_GROUP_WIDTH)
        yg = jnp.concatenate(ys, axis=0)
        yg = yg * lax.rsqrt(jnp.mean(yg * yg, axis=0, keepdims=True) + EPS) * nw_ref[gch, :]
        o_ref[c, gch, :] = yg.astype(o_ref.dtype)
        upd = jnp.dot(jnp.concatenate(xds, axis=0), bm, preferred_element_type=F32)
        for r in range(S_HEADS_PER_GROUP):
            i = g * S_HEADS_PER_GROUP + r
            hs = slice(r * S_HEADDIM, (r + 1) * S_HEADDIM)
            h_ref[g, hs, :] = chunk_decay[i:i + 1, :] * h_prev[hs, :] + upd[hs, :]


def _mixer_kernel(q_ref, k_ref, vt_ref, ogt_ref, zst_ref, xst_ref, bm_ref, cm_ref, gt_ref,
                  mnw_ref, snw_ref, dskip_ref, alog_ref,
                  hm_ref, y_ref, ct_ref, m_ref, h_ref):
    @pl.when(pl.program_id(1) == 0)
    def _():
        ct_ref[...] = jnp.zeros(ct_ref.shape, F32)
        m_ref[...] = jnp.zeros(m_ref.shape, F32)
        h_ref[...] = jnp.zeros(h_ref.shape, F32)

    r = lax.broadcasted_iota(jnp.int32, (CHUNK, CHUNK), 0)
    l = lax.broadcasted_iota(jnp.int32, (CHUNK, CHUNK), 1)
    causal = r <= l
    upper = jnp.where(causal, 1.0, 0.0).astype(F32)
    for c in range(vt_ref.shape[0]):
        rows = slice(c * CHUNK, (c + 1) * CHUNK)
        gates = gt_ref[c]
        _mlstm_chunk(c, rows, gates, upper, causal, q_ref, k_ref, vt_ref, ogt_ref, mnw_ref,
                     hm_ref, ct_ref, m_ref)
        _ssd_chunk(c, rows, gates, upper, causal, xst_ref, bm_ref, cm_ref, zst_ref, alog_ref,
                   dskip_ref, snw_ref, y_ref, h_ref)


def _mixer(q, k, vt, ogt, zst, xst, bm, cm, gt, mnw_x, snw_x, dskip_x, alog_x):
    bsz, n_chunks = vt.shape[0], vt.shape[1]
    cps = CHUNKS_PER_STEP
    grid = (bsz, n_chunks // cps)
    rowmajor = lambda w: pl.BlockSpec((None, cps * CHUNK, w), lambda b, s: (b, s, 0))
    transposed = lambda ch: pl.BlockSpec((None, cps, ch, CHUNK), lambda b, s: (b, s, 0, 0))
    fixed = lambda a: pl.BlockSpec(a.shape, lambda b, s: (0, 0))
    out_sds = jax.ShapeDtypeStruct((bsz, n_chunks, M_WIDTH, CHUNK), BF16)
    return pl.pallas_call(
        _mixer_kernel,
        grid=grid,
        in_specs=[rowmajor(M_QK), rowmajor(M_QK), transposed(M_WIDTH), transposed(M_WIDTH),
                  transposed(S_WIDTH), transposed(S_WIDTH),
                  rowmajor(S_GROUPS * S_STATE), rowmajor(S_GROUPS * S_STATE),
                  transposed(G_ROWS),
                  fixed(mnw_x), fixed(snw_x), fixed(dskip_x), fixed(alog_x)],
        out_specs=(transposed(M_WIDTH), transposed(S_WIDTH)),
        out_shape=(out_sds, out_sds),
        scratch_shapes=[
            pltpu.VMEM((M_HEADS, M_DV + BF16_ROWS, M_DQK), F32),
            pltpu.VMEM((M_HEADS, 1, LANES), F32),
            pltpu.VMEM((S_GROUPS, S_GROUP_WIDTH, S_STATE), F32),
        ],
        compiler_params=pltpu.CompilerParams(
            dimension_semantics=("arbitrary", "arbitrary"),
            vmem_limit_bytes=VMEM_LIMIT_BYTES),
        name="mixer",
    )(q, k, vt, ogt, zst, xst, bm, cm, gt, mnw_x, snw_x, dskip_x, alog_x)


def _out_proj_kernel(x_ref, hm_ref, y_ref, wm_ref, ws_ref, nw_ref, o_ref):
    n_chunks = hm_ref.shape[0]
    hm = jnp.concatenate([hm_ref[c] for c in range(n_chunks)], axis=1)
    y = jnp.concatenate([y_ref[c] for c in range(n_chunks)], axis=1)
    acc = x_ref[...] + lax.dot_general(hm, wm_ref[...], TN, preferred_element_type=F32)
    acc = acc + lax.dot_general(y, ws_ref[...], TN, preferred_element_type=F32)
    ms = jnp.mean(acc * acc, axis=-1, keepdims=True)
    o_ref[...] = acc * lax.rsqrt(ms + EPS) * nw_ref[...]


def _out_proj(x2, hm, y, w_m, w_s, nw, *, bsz, seq):
    tm = TM_OUT
    tiles_per_seq = seq // tm
    cpt = tm // CHUNK
    row = lambda i: (i, 0)
    fixed = lambda i: (0, 0)
    chunked = lambda i: (i // tiles_per_seq, i % tiles_per_seq, 0, 0)
    return pl.pallas_call(
        _out_proj_kernel,
        grid=(bsz * tiles_per_seq,),
        in_specs=[
            pl.BlockSpec((tm, D_MODEL), row),
            pl.BlockSpec((None, cpt, M_WIDTH, CHUNK), chunked),
            pl.BlockSpec((None, cpt, S_WIDTH, CHUNK), chunked),
            pl.BlockSpec((M_WIDTH, D_MODEL), fixed),
            pl.BlockSpec((S_WIDTH, D_MODEL), fixed),
            pl.BlockSpec((1, D_MODEL), fixed),
        ],
        out_specs=pl.BlockSpec((tm, D_MODEL), row),
        out_shape=jax.ShapeDtypeStruct((bsz * seq, D_MODEL), F32),
        compiler_params=pltpu.CompilerParams(
            dimension_semantics=("arbitrary",),
            vmem_limit_bytes=VMEM_LIMIT_BYTES),
        name="out_proj",
    )(x2, hm, y, w_m, w_s, nw)


def _lane_expand(v):
    return jnp.broadcast_to(v.astype(F32).reshape(-1, 1), (v.size, LANES))


def _layer(x, norm_w, w_in, b_igate, b_fgate, conv_w, conv_b, dt_bias, a_log, d_skip,
           mlstm_norm_w, ssd_norm_w, w_out, out_norm_w):
    bsz, seq, _ = x.shape
    assert seq % max(TM_IN, TM_OUT, CHUNK * CHUNKS_PER_STEP) == 0
    n_tok = bsz * seq
    x2 = x.reshape(n_tok, D_MODEL)

    edges = [0]
    for w in (M_QK, M_QK, M_WIDTH, M_HEADS, M_HEADS, M_WIDTH, M_WIDTH, S_WIDTH, S_CONV_CH, S_HEADS):
        edges.append(edges[-1] + w)
    seg = lambda j: w_in[:, edges[j]:edges[j + 1]]
    w_n = jnp.concatenate([seg(0), seg(1), seg(8)], axis=1).astype(BF16)
    w_t = jnp.concatenate([seg(2), seg(5), seg(6), seg(7)], axis=1).T.astype(BF16)
    w_g = jnp.concatenate([seg(3), seg(4), seg(9)], axis=1).T
    w_g = jnp.pad(w_g, ((0, G_ROWS - w_g.shape[0]), (0, 0))).astype(BF16)
    gate_bias = jnp.concatenate([b_igate, b_fgate, dt_bias]).astype(F32)
    gate_bias = jnp.pad(gate_bias, (0, G_ROWS - gate_bias.shape[0])).reshape(G_ROWS, 1)

    q, k, vt, ogt, zst, xst, bm, cm, gt = _in_proj(
        x2, norm_w.reshape(1, D_MODEL), w_n, w_t, w_g, gate_bias,
        conv_w.T.astype(F32), conv_b.reshape(1, S_CONV_CH).astype(F32), bsz=bsz, seq=seq)

    r3 = lambda a: a.reshape(bsz, seq, a.shape[-1])
    hm, y = _mixer(r3(q), r3(k), vt, ogt, zst, xst, r3(bm), r3(cm), gt,
                   _lane_expand(mlstm_norm_w), _lane_expand(ssd_norm_w),
                   _lane_expand(jnp.repeat(d_skip, S_HEADDIM)),
                   jnp.broadcast_to(a_log.astype(F32).reshape(S_HEADS, 1), (S_HEADS, CHUNK)))

    w_o = w_out.astype(BF16)
    out = _out_proj(x2, hm, y, w_o[:M_WIDTH], w_o[M_WIDTH:], out_norm_w.reshape(1, D_MODEL),
                    bsz=bsz, seq=seq)
    return out.reshape(bsz, seq, D_MODEL)


def kernel(x, norm_w, w_in, b_igate, b_fgate, conv_w, conv_b, dt_bias, a_log, d_skip,
           mlstm_norm_w, ssd_norm_w, w_out, final_norm_w):
    depth = norm_w.shape[0]
    assert depth == 1, "final RMSNorm is fused into the single layer's output projection"
    return _layer(x, norm_w[0], w_in[0], b_igate[0], b_fgate[0], conv_w[0], conv_b[0],
                  dt_bias[0], a_log[0], d_skip[0], mlstm_norm_w[0], ssd_norm_w[0], w_out[0],
                  final_norm_w)
```

```python
import functools
import math

import jax
import jax.numpy as jnp
from jax import lax
from jax.experimental import pallas as pl
from jax.experimental.pallas import tpu as pltpu

F32 = jnp.float32
BF16 = jnp.bfloat16

D_MODEL = 1024
D_INNER = 2 * D_MODEL
M_WIDTH = D_INNER // 2
M_HEADS = 4
M_DV = M_WIDTH // M_HEADS
M_DQK = M_DV // 2
M_QK = M_HEADS * M_DQK
CHUNK = 128
GATE_CAP = 15.0
S_WIDTH = D_INNER - M_WIDTH
S_HEADDIM = 64
S_HEADS = S_WIDTH // S_HEADDIM
S_GROUPS = 2
S_HEADS_PER_GROUP = S_HEADS // S_GROUPS
S_GROUP_WIDTH = S_WIDTH // S_GROUPS
S_STATE = 128
S_CONV = 4
S_CONV_CH = S_WIDTH + 2 * S_GROUPS * S_STATE
EPS = 1e-6
LOG2E = math.log2(math.e)

LANES = 128
SUBLANES = 8
BF16_ROWS = 16
VMEM_LIMIT_BYTES = 56 * 1024 * 1024

OFF_K = 0
OFF_XBC = OFF_K + M_QK
N_ROWMAJOR = OFF_XBC + S_CONV_CH
OFF_Q = 0
OFF_V = OFF_Q + M_QK
OFF_O = OFF_V + M_WIDTH
OFF_ZM = OFF_O + M_WIDTH
OFF_ZS = OFF_ZM + M_WIDTH
N_TRANSPOSED = OFF_ZS + S_WIDTH
G_I = 0
G_F = G_I + M_HEADS
G_DT = G_F + M_HEADS
G_END = G_DT + S_HEADS
G_ROWS = 32

NN_BLOCK = 256
NT_BLOCK = 512
_IN_PROJ_SCHEDULE = "x0 k0 x1 q0 x2 k1 x3 v0 o0 x4 v1 o1 x5 z0 z1 g0".split()
TM_IN = 512
TM_OUT = 512
CHUNKS_PER_STEP = 4

NT = (((1,), (1,)), ((), ()))
TN = (((0,), (0,)), ((), ()))


def _sigmoid(u):
    return 1.0 / (1.0 + jnp.exp(-u))


def _silu(u):
    return u * _sigmoid(u)


def _softplus(u):
    return jnp.maximum(u, 0.0) + jnp.log1p(jnp.exp(-jnp.abs(u)))


def _in_proj_kernel(x_ref, nw_ref, wn_ref, wt_ref, wg_ref, gb_ref, cw_ref, cb_ref, mnw_ref,
                    qt_ref, k_ref, vt_ref, ogt_ref, zst_ref, xst_ref, bm_ref, cm_ref, gt_ref,
                    xn_ref, cbuf_ref, *, tiles_per_seq):
    tm = x_ref.shape[0]
    n_chunks = tm // CHUNK
    x = x_ref[...]
    ms = jnp.mean(x * x, axis=-1, keepdims=True)
    xn_ref[...] = (x * lax.rsqrt(ms + EPS) * nw_ref[...]).astype(BF16)

    def proj(lo, width):
        return jnp.dot(xn_ref[...], wn_ref[:, lo:lo + width], preferred_element_type=F32)

    def proj_t(lo, width):
        return lax.dot_general(wt_ref[lo:lo + width, :], xn_ref[...], NT,
                               preferred_element_type=F32)

    def store_chunks(ref, lo, val):
        for c in range(n_chunks):
            ref[c, lo:lo + val.shape[0], :] = val[:, c * CHUNK:(c + 1) * CHUNK].astype(ref.dtype)

    @pl.when(pl.program_id(0) % tiles_per_seq == 0)
    def _():
        cbuf_ref[0:SUBLANES, :] = jnp.zeros((SUBLANES, S_CONV_CH), F32)

    def xbc_block(j):
        lo = j * NN_BLOCK
        cols = slice(lo, lo + NN_BLOCK)
        cbuf_ref[SUBLANES:SUBLANES + tm, cols] = proj(OFF_XBC + lo, NN_BLOCK)
        acc = (cb_ref[:, cols]
               + cw_ref[S_CONV - 1:S_CONV, cols] * cbuf_ref[SUBLANES:SUBLANES + tm, cols])
        for t in range(1, S_CONV):
            acc = acc + (cw_ref[S_CONV - 1 - t:S_CONV - t, cols]
                         * cbuf_ref[SUBLANES - t:SUBLANES - t + tm, cols])
        cbuf_ref[0:SUBLANES, cols] = cbuf_ref[tm:tm + SUBLANES, cols]
        val = _silu(acc)
        if lo < S_WIDTH:
            store_chunks(xst_ref, lo, val.T)
        elif lo < S_WIDTH + S_GROUPS * S_STATE:
            bm_ref[:, lo - S_WIDTH:lo - S_WIDTH + NN_BLOCK] = val.astype(BF16)
        else:
            off = lo - S_WIDTH - S_GROUPS * S_STATE
            cm_ref[:, off:off + NN_BLOCK] = val.astype(BF16)

    def k_block(j):
        lo = j * NN_BLOCK
        k_ref[:, lo:lo + NN_BLOCK] = proj(OFF_K + lo, NN_BLOCK).astype(BF16)

    def qt_block(j):
        lo = j * NT_BLOCK
        store_chunks(qt_ref, lo, proj_t(OFF_Q + lo, NT_BLOCK) * (M_DQK ** -0.5))

    def vt_block(j):
        lo = j * NT_BLOCK
        store_chunks(vt_ref, lo, proj_t(OFF_V + lo, NT_BLOCK))

    def ogt_block(j):
        lo = j * NT_BLOCK
        gate = _sigmoid(proj_t(OFF_O + lo, NT_BLOCK)) * _silu(proj_t(OFF_ZM + lo, NT_BLOCK))
        store_chunks(ogt_ref, lo, gate * mnw_ref[lo:lo + NT_BLOCK, :])

    def zst_block(j):
        lo = j * NT_BLOCK
        store_chunks(zst_ref, lo, _silu(proj_t(OFF_ZS + lo, NT_BLOCK)))

    def gate_block(_):
        g = (lax.dot_general(wg_ref[...], xn_ref[...], NT, preferred_element_type=F32)
             + gb_ref[...])
        row = lax.broadcasted_iota(jnp.int32, g.shape, 0)
        capped = GATE_CAP * jnp.tanh(g / GATE_CAP)
        log_f = jnp.minimum(capped, 0.0) - jnp.log1p(jnp.exp(-jnp.abs(capped)))
        gates = jnp.where(row < G_F, capped,
                          jnp.where(row < G_DT, log_f,
                                    jnp.where(row < G_END, _softplus(g), 0.0)))
        store_chunks(gt_ref, 0, gates)

    blocks = {"x": xbc_block, "k": k_block, "q": qt_block, "v": vt_block, "o": ogt_block,
              "z": zst_block, "g": gate_block}
    for name in _IN_PROJ_SCHEDULE:
        blocks[name[0]](int(name[1:]))


def _in_proj(x2, nw, w_n, w_t, w_g, gate_bias, conv_w_t, conv_b, mnw_col, *, bsz, seq):
    tm = TM_IN
    tiles_per_seq = seq // tm
    n_chunks = seq // CHUNK
    cpt = tm // CHUNK
    grid = (bsz * tiles_per_seq,)
    row = lambda i: (i, 0)
    fixed = lambda i: (0, 0)
    chunked = lambda i: (i // tiles_per_seq, i % tiles_per_seq, 0, 0)

    def rowmajor(width):
        return (jax.ShapeDtypeStruct((bsz * seq, width), BF16), pl.BlockSpec((tm, width), row))

    def transposed(channels, dtype):
        return (jax.ShapeDtypeStruct((bsz, n_chunks, channels, CHUNK), dtype),
                pl.BlockSpec((None, cpt, channels, CHUNK), chunked))

    outs = [transposed(M_QK, BF16), rowmajor(M_QK),
            transposed(M_WIDTH, BF16), transposed(M_WIDTH, BF16),
            transposed(S_WIDTH, BF16), transposed(S_WIDTH, BF16),
            rowmajor(S_GROUPS * S_STATE), rowmajor(S_GROUPS * S_STATE),
            transposed(G_ROWS, F32)]
    return pl.pallas_call(
        functools.partial(_in_proj_kernel, tiles_per_seq=tiles_per_seq),
        grid=grid,
        in_specs=[
            pl.BlockSpec((tm, D_MODEL), row),
            pl.BlockSpec((1, D_MODEL), fixed),
            pl.BlockSpec((D_MODEL, N_ROWMAJOR), fixed),
            pl.BlockSpec((N_TRANSPOSED, D_MODEL), fixed),
            pl.BlockSpec((G_ROWS, D_MODEL), fixed),
            pl.BlockSpec((G_ROWS, 1), fixed),
            pl.BlockSpec((S_CONV, S_CONV_CH), fixed),
            pl.BlockSpec((1, S_CONV_CH), fixed),
            pl.BlockSpec((M_WIDTH, 1), fixed),
        ],
        out_specs=tuple(o[1] for o in outs),
        out_shape=tuple(o[0] for o in outs),
        scratch_shapes=[
            pltpu.VMEM((tm, D_MODEL), BF16),
            pltpu.VMEM((tm + SUBLANES, S_CONV_CH), F32),
        ],
        compiler_params=pltpu.CompilerParams(
            dimension_semantics=("arbitrary",),
            vmem_limit_bytes=VMEM_LIMIT_BYTES),
        name="in_proj",
    )(x2, nw, w_n, w_t, w_g, gate_bias, conv_w_t, conv_b, mnw_col)


def _cumsum_lanes(rows, upper):
    return jnp.dot(rows, upper, preferred_element_type=F32, precision=lax.Precision.HIGHEST)


def _column_of(row):
    return jnp.broadcast_to(row, (CHUNK, CHUNK)).T


def _mlstm_head(h, c, b_all, u_all, causal, qt_ref, k_ref, vt_ref, ogt_ref, o_ref, ct_ref, m_ref):
    rows = slice(c * CHUNK, (c + 1) * CHUNK)
    ch = slice(h * M_DV, (h + 1) * M_DV)
    qt = qt_ref[c, h * M_DQK:(h + 1) * M_DQK, :]
    k = k_ref[rows, h * M_DQK:(h + 1) * M_DQK]
    ones = jnp.ones((BF16_ROWS, CHUNK), BF16)
    vt_aug = jnp.concatenate([vt_ref[c, ch, :], ones], axis=0)
    b = b_all[h:h + 1, :]
    u_col = _column_of(u_all[h:h + 1, :])
    u_msk = jnp.where(causal, u_col, -jnp.inf)
    u_max = jnp.max(u_msk, axis=0, keepdims=True)
    qk = jnp.dot(k, qt, preferred_element_type=F32)
    yield

    m_prev = m_ref[h][:, 0:1]
    ct_prev = ct_ref[h]
    mm = jnp.maximum(m_prev, u_max)
    st = qk * jnp.exp2(u_msk - mm)
    inter_scale = jnp.exp2(m_prev - mm)
    lhs = jnp.concatenate([vt_aug, ct_prev.astype(BF16)], axis=1)
    rhs = jnp.concatenate([st.astype(BF16), (qt.astype(F32) * inter_scale).astype(BF16)], axis=0)
    num_aug = jnp.dot(lhs, rhs, preferred_element_type=F32)
    mm_last = mm[:, CHUNK - 1:CHUNK]
    kw = (k.astype(F32) * jnp.exp2(u_col - mm_last)).astype(BF16)
    ct_ref[h] = (inter_scale[:, CHUNK - 1:CHUNK] * ct_prev
                 + jnp.dot(vt_aug, kw, preferred_element_type=F32))
    m_ref[h] = jnp.broadcast_to(b[:, CHUNK - 1:CHUNK] + mm_last, (1, LANES))
    yield

    num = num_aug[0:M_DV, :]
    den = num_aug[M_DV:M_DV + 1, :]
    rden = 1.0 / jnp.maximum(jnp.abs(den), jnp.exp2(-(b + mm)))
    msq = jnp.mean(num * num, axis=0, keepdims=True)
    scale = rden * lax.rsqrt(rden * rden * msq + EPS)
    o_ref[c, ch, :] = (num * scale).astype(BF16) * ogt_ref[c, ch, :]


def _ssd_group(g, c, dt, acum, ea, decay_end, chunk_decay, causal, xst_ref, bm_ref, cm_ref,
               zst_ref, dskip_ref, nw_ref, o_ref, h_ref):
    rows = slice(c * CHUNK, (c + 1) * CHUNK)
    bm = bm_ref[rows, g * S_STATE:(g + 1) * S_STATE]
    cm = cm_ref[rows, g * S_STATE:(g + 1) * S_STATE]
    cbt = lax.dot_general(bm, cm, NT, preferred_element_type=F32)
    yield
    yds, xds = [], []
    for r in range(S_HEADS_PER_GROUP):
        i = g * S_HEADS_PER_GROUP + r
        ch = slice(i * S_HEADDIM, (i + 1) * S_HEADDIM)
        a_row = acum[i:i + 1, :]
        decay = jnp.exp2(jnp.where(causal, a_row - _column_of(a_row), -jnp.inf))
        mt = (cbt * decay).astype(BF16)
        xdt = xst_ref[c, ch, :].astype(F32) * dt[i:i + 1, :]
        yds.append(jnp.dot(xdt.astype(BF16), mt, preferred_element_type=F32))
        xds.append((xdt * decay_end[i:i + 1, :]).astype(BF16))
        yield

    h_prev = h_ref[g]
    y_off = lax.dot_general(h_prev.astype(BF16), cm, NT, preferred_element_type=F32)
    upd = jnp.dot(jnp.concatenate(xds, axis=0), bm, preferred_element_type=F32)
    for r in range(S_HEADS_PER_GROUP):
        i = g * S_HEADS_PER_GROUP + r
        hs = slice(r * S_HEADDIM, (r + 1) * S_HEADDIM)
        h_ref[g, hs, :] = chunk_decay[i:i + 1, :] * h_prev[hs, :] + upd[hs, :]
    yield

    ys = []
    for r in range(S_HEADS_PER_GROUP):
        i = g * S_HEADS_PER_GROUP + r
        ch = slice(i * S_HEADDIM, (i + 1) * S_HEADDIM)
        y = (yds[r] + y_off[r * S_HEADDIM:(r + 1) * S_HEADDIM, :] * ea[i:i + 1, :]
             + dskip_ref[ch, :] * xst_ref[c, ch, :].astype(F32))
        ys.append(y * zst_ref[c, ch, :].astype(F32))
    gch = slice(g * S_GROUP_WIDTH, (g + 1) * S_GROUP_WIDTH)
    yg = jnp.concatenate(ys, axis=0)
    yg = yg * lax.rsqrt(jnp.mean(yg * yg, axis=0, keepdims=True) + EPS) * nw_ref[gch, :]
    o_ref[c, gch, :] = yg.astype(o_ref.dtype)


def _interleave(a, b):
    if len(a) < len(b):
        a, b = b, a
    out, taken = [], 0
    for n, item in enumerate(a):
        out.append(item)
        want = ((n + 1) * len(b)) // len(a)
        out.extend(b[taken:want])
        taken = want
    return out


def _mixer_kernel(qt_ref, k_ref, vt_ref, ogt_ref, zst_ref, xst_ref, bm_ref, cm_ref, gt_ref,
                  snw_ref, dskip_ref, alog_ref,
                  hm_ref, y_ref, ct_ref, m_ref, h_ref):
    @pl.when(pl.program_id(1) == 0)
    def _():
        ct_ref[...] = jnp.zeros(ct_ref.shape, F32)
        m_ref[...] = jnp.zeros(m_ref.shape, F32)
        h_ref[...] = jnp.zeros(h_ref.shape, F32)

    r = lax.broadcasted_iota(jnp.int32, (CHUNK, CHUNK), 0)
    l = lax.broadcasted_iota(jnp.int32, (CHUNK, CHUNK), 1)
    causal = r <= l
    upper = jnp.where(causal, 1.0, 0.0).astype(F32)
    a_neg = -LOG2E * jnp.exp(alog_ref[...])

    def chunk_tasks(c):
        gates = gt_ref[c]
        cum = _cumsum_lanes(gates[0:SUBLANES, :] * LOG2E, upper)
        b_all = cum[G_F:G_F + M_HEADS, :]
        u_all = gates[G_I:G_I + M_HEADS, :] * LOG2E - b_all
        dt = gates[G_DT:G_END, :]
        acum = _cumsum_lanes(dt * a_neg, upper)
        a_last = acum[:, CHUNK - 1:CHUNK]
        ea, decay_end, chunk_decay = jnp.exp2(acum), jnp.exp2(a_last - acum), jnp.exp2(a_last)
        ml = [_mlstm_head(h, c, b_all, u_all, causal, qt_ref, k_ref, vt_ref, ogt_ref,
                          hm_ref, ct_ref, m_ref) for h in range(M_HEADS)]
        sg = [_ssd_group(g, c, dt, acum, ea, decay_end, chunk_decay, causal, xst_ref, bm_ref,
                         cm_ref, zst_ref, dskip_ref, snw_ref, y_ref, h_ref)
              for g in range(S_GROUPS)]
        n_s = 1 + S_HEADS_PER_GROUP
        early = _interleave([s for _ in range(n_s) for s in sg], ml)
        late = _interleave(sg + ml + sg, ml)
        return early, late

    n = vt_ref.shape[0]
    prev_late = []
    for c in range(n):
        early, late = chunk_tasks(c)
        for t in _interleave(early, prev_late):
            next(t, None)
        prev_late = late
    for t in prev_late:
        next(t, None)


def _mixer(qt, k, vt, ogt, zst, xst, bm, cm, gt, snw_x, dskip_x, alog_x):
    bsz, n_chunks = vt.shape[0], vt.shape[1]
    cps = CHUNKS_PER_STEP
    grid = (bsz, n_chunks // cps)
    rowmajor = lambda w: pl.BlockSpec((None, cps * CHUNK, w), lambda b, s: (b, s, 0))
    transposed = lambda ch: pl.BlockSpec((None, cps, ch, CHUNK), lambda b, s: (b, s, 0, 0))
    fixed = lambda a: pl.BlockSpec(a.shape, lambda b, s: (0, 0))
    out_sds = jax.ShapeDtypeStruct((bsz, n_chunks, M_WIDTH, CHUNK), BF16)
    return pl.pallas_call(
        _mixer_kernel,
        grid=grid,
        in_specs=[transposed(M_QK), rowmajor(M_QK), transposed(M_WIDTH), transposed(M_WIDTH),
                  transposed(S_WIDTH), transposed(S_WIDTH),
                  rowmajor(S_GROUPS * S_STATE), rowmajor(S_GROUPS * S_STATE),
                  transposed(G_ROWS),
                  fixed(snw_x), fixed(dskip_x), fixed(alog_x)],
        out_specs=(transposed(M_WIDTH), transposed(S_WIDTH)),
        out_shape=(out_sds, out_sds),
        scratch_shapes=[
            pltpu.VMEM((M_HEADS, M_DV + BF16_ROWS, M_DQK), F32),
            pltpu.VMEM((M_HEADS, 1, LANES), F32),
            pltpu.VMEM((S_GROUPS, S_GROUP_WIDTH, S_STATE), F32),
        ],
        compiler_params=pltpu.CompilerParams(
            dimension_semantics=("arbitrary", "arbitrary"),
            vmem_limit_bytes=VMEM_LIMIT_BYTES),
        name="mixer",
    )(qt, k, vt, ogt, zst, xst, bm, cm, gt, snw_x, dskip_x, alog_x)


def _out_proj_kernel(x_ref, hm_ref, y_ref, wm_ref, ws_ref, nw_ref, o_ref):
    n_chunks = hm_ref.shape[0]
    hm = jnp.concatenate([hm_ref[c] for c in range(n_chunks)], axis=1)
    y = jnp.concatenate([y_ref[c] for c in range(n_chunks)], axis=1)
    acc = x_ref[...] + lax.dot_general(hm, wm_ref[...], TN, preferred_element_type=F32)
    acc = acc + lax.dot_general(y, ws_ref[...], TN, preferred_element_type=F32)
    ms = jnp.mean(acc * acc, axis=-1, keepdims=True)
    o_ref[...] = acc * lax.rsqrt(ms + EPS) * nw_ref[...]


def _out_proj(x2, hm, y, w_m, w_s, nw, *, bsz, seq):
    tm = TM_OUT
    tiles_per_seq = seq // tm
    cpt = tm // CHUNK
    row = lambda i: (i, 0)
    fixed = lambda i: (0, 0)
    chunked = lambda i: (i // tiles_per_seq, i % tiles_per_seq, 0, 0)
    return pl.pallas_call(
        _out_proj_kernel,
        grid=(bsz * tiles_per_seq,),
        in_specs=[
            pl.BlockSpec((tm, D_MODEL), row),
            pl.BlockSpec((None, cpt, M_WIDTH, CHUNK), chunked),
            pl.BlockSpec((None, cpt, S_WIDTH, CHUNK), chunked),
            pl.BlockSpec((M_WIDTH, D_MODEL), fixed),
            pl.BlockSpec((S_WIDTH, D_MODEL), fixed),
            pl.BlockSpec((1, D_MODEL), fixed),
        ],
        out_specs=pl.BlockSpec((tm, D_MODEL), row),
        out_shape=jax.ShapeDtypeStruct((bsz * seq, D_MODEL), F32),
        compiler_params=pltpu.CompilerParams(
            dimension_semantics=("arbitrary",),
            vmem_limit_bytes=VMEM_LIMIT_BYTES),
        name="out_proj",
    )(x2, hm, y, w_m, w_s, nw)


def _lane_expand(v):
    return jnp.broadcast_to(v.astype(F32).reshape(-1, 1), (v.size, LANES))


def _layer(x, norm_w, w_in, b_igate, b_fgate, conv_w, conv_b, dt_bias, a_log, d_skip,
           mlstm_norm_w, ssd_norm_w, w_out, out_norm_w):
    bsz, seq, _ = x.shape
    assert seq % max(TM_IN, TM_OUT, CHUNK * CHUNKS_PER_STEP) == 0
    n_tok = bsz * seq
    x2 = x.reshape(n_tok, D_MODEL)

    edges = [0]
    for w in (M_QK, M_QK, M_WIDTH, M_HEADS, M_HEADS, M_WIDTH, M_WIDTH, S_WIDTH, S_CONV_CH, S_HEADS):
        edges.append(edges[-1] + w)
    seg = lambda j: w_in[:, edges[j]:edges[j + 1]]
    w_n = jnp.concatenate([seg(1), seg(8)], axis=1).astype(BF16)
    w_t = jnp.concatenate([seg(0), seg(2), seg(5), seg(6), seg(7)], axis=1).T.astype(BF16)
    w_g = jnp.concatenate([seg(3), seg(4), seg(9)], axis=1).T
    w_g = jnp.pad(w_g, ((0, G_ROWS - w_g.shape[0]), (0, 0))).astype(BF16)
    gate_bias = jnp.concatenate([b_igate, b_fgate, dt_bias]).astype(F32)
    gate_bias = jnp.pad(gate_bias, (0, G_ROWS - gate_bias.shape[0])).reshape(G_ROWS, 1)

    qt, k, vt, ogt, zst, xst, bm, cm, gt = _in_proj(
        x2, norm_w.reshape(1, D_MODEL), w_n, w_t, w_g, gate_bias,
        conv_w.T.astype(F32), conv_b.reshape(1, S_CONV_CH).astype(F32),
        mlstm_norm_w.astype(F32).reshape(M_WIDTH, 1), bsz=bsz, seq=seq)

    r3 = lambda a: a.reshape(bsz, seq, a.shape[-1])
    hm, y = _mixer(qt, r3(k), vt, ogt, zst, xst, r3(bm), r3(cm), gt,
                   _lane_expand(ssd_norm_w), _lane_expand(jnp.repeat(d_skip, S_HEADDIM)),
                   jnp.broadcast_to(a_log.astype(F32).reshape(S_HEADS, 1), (S_HEADS, CHUNK)))

    w_o = w_out.astype(BF16)
    out = _out_proj(x2, hm, y, w_o[:M_WIDTH], w_o[M_WIDTH:], out_norm_w.reshape(1, D_MODEL),
                    bsz=bsz, seq=seq)
    return out.reshape(bsz, seq, D_MODEL)


def kernel(x, norm_w, w_in, b_igate, b_fgate, conv_w, conv_b, dt_bias, a_log, d_skip,
           mlstm_norm_w, ssd_norm_w, w_out, final_norm_w):
    depth = norm_w.shape[0]
    assert depth == 1, "final RMSNorm is fused into the single layer's output projection"
    return _layer(x, norm_w[0], w_in[0], b_igate[0], b_fgate[0], conv_w[0], conv_b[0],
                  dt_bias[0], a_log[0], d_skip[0], mlstm_norm_w[0], ssd_norm_w[0], w_out[0],
                  final_norm_w)
```

```python
import functools
import math

import jax
import jax.numpy as jnp
from jax import lax
from jax.experimental import pallas as pl
from jax.experimental.pallas import tpu as pltpu

F32 = jnp.float32
BF16 = jnp.bfloat16

D_MODEL = 1024
D_INNER = 2 * D_MODEL
M_WIDTH = D_INNER // 2
M_HEADS = 4
M_DV = M_WIDTH // M_HEADS
M_DQK = M_DV // 2
M_QK = M_HEADS * M_DQK
CHUNK = 128
GATE_CAP = 15.0
S_WIDTH = D_INNER - M_WIDTH
S_HEADDIM = 64
S_HEADS = S_WIDTH // S_HEADDIM
S_GROUPS = 2
S_HEADS_PER_GROUP = S_HEADS // S_GROUPS
S_GROUP_WIDTH = S_WIDTH // S_GROUPS
S_STATE = 128
S_CONV = 4
S_CONV_CH = S_WIDTH + 2 * S_GROUPS * S_STATE
EPS = 1e-6
LOG2E = math.log2(math.e)

LANES = 128
SUBLANES = 8
BF16_ROWS = 16
VMEM_LIMIT_BYTES = 56 * 1024 * 1024

OFF_K = 0
OFF_XBC = OFF_K + M_QK
N_ROWMAJOR = OFF_XBC + S_CONV_CH
OFF_Q = 0
OFF_V = OFF_Q + M_QK
OFF_O = OFF_V + M_WIDTH
OFF_ZM = OFF_O + M_WIDTH
OFF_ZS = OFF_ZM + M_WIDTH
N_TRANSPOSED = OFF_ZS + S_WIDTH
G_I = 0
G_F = G_I + M_HEADS
G_DT = G_F + M_HEADS
G_END = G_DT + S_HEADS
G_ROWS = 32

NN_BLOCK = 256
NT_BLOCK = 512
_IN_PROJ_SCHEDULE = "x0 k0 x1 q0 x2 k1 x3 v0 o0 x4 v1 o1 x5 z0 z1 g0".split()
TM_IN = 512
CHUNKS_PER_STEP = 4
OUT_BLOCK = 256

NT = (((1,), (1,)), ((), ()))
TN = (((0,), (0,)), ((), ()))


def _sigmoid(u):
    return 1.0 / (1.0 + jnp.exp(-u))


def _silu(u):
    return u * _sigmoid(u)


def _softplus(u):
    return jnp.maximum(u, 0.0) + jnp.log1p(jnp.exp(-jnp.abs(u)))


def _in_proj_kernel(x_ref, nw_ref, wn_ref, wt_ref, wg_ref, gb_ref, cw_ref, cb_ref, mnw_ref,
                    qt_ref, k_ref, vt_ref, ogt_ref, zst_ref, xst_ref, bm_ref, cm_ref, gt_ref,
                    xn_ref, cbuf_ref, *, tiles_per_seq):
    tm = x_ref.shape[0]
    n_chunks = tm // CHUNK
    x = x_ref[...]
    ms = jnp.mean(x * x, axis=-1, keepdims=True)
    xn_ref[...] = (x * lax.rsqrt(ms + EPS) * nw_ref[...]).astype(BF16)

    def proj(lo, width):
        return jnp.dot(xn_ref[...], wn_ref[:, lo:lo + width], preferred_element_type=F32)

    def proj_t(lo, width):
        return lax.dot_general(wt_ref[lo:lo + width, :], xn_ref[...], NT,
                               preferred_element_type=F32)

    def store_chunks(ref, lo, val):
        for c in range(n_chunks):
            ref[c, lo:lo + val.shape[0], :] = val[:, c * CHUNK:(c + 1) * CHUNK].astype(ref.dtype)

    @pl.when(pl.program_id(0) % tiles_per_seq == 0)
    def _():
        cbuf_ref[0:SUBLANES, :] = jnp.zeros((SUBLANES, S_CONV_CH), F32)

    def xbc_block(j):
        lo = j * NN_BLOCK
        cols = slice(lo, lo + NN_BLOCK)
        cbuf_ref[SUBLANES:SUBLANES + tm, cols] = proj(OFF_XBC + lo, NN_BLOCK)
        acc = (cb_ref[:, cols]
               + cw_ref[S_CONV - 1:S_CONV, cols] * cbuf_ref[SUBLANES:SUBLANES + tm, cols])
        for t in range(1, S_CONV):
            acc = acc + (cw_ref[S_CONV - 1 - t:S_CONV - t, cols]
                         * cbuf_ref[SUBLANES - t:SUBLANES - t + tm, cols])
        cbuf_ref[0:SUBLANES, cols] = cbuf_ref[tm:tm + SUBLANES, cols]
        val = _silu(acc)
        if lo < S_WIDTH:
            store_chunks(xst_ref, lo, val.T)
        elif lo < S_WIDTH + S_GROUPS * S_STATE:
            bm_ref[:, lo - S_WIDTH:lo - S_WIDTH + NN_BLOCK] = val.astype(BF16)
        else:
            off = lo - S_WIDTH - S_GROUPS * S_STATE
            cm_ref[:, off:off + NN_BLOCK] = val.astype(BF16)

    def k_block(j):
        lo = j * NN_BLOCK
        k_ref[:, lo:lo + NN_BLOCK] = proj(OFF_K + lo, NN_BLOCK).astype(BF16)

    def qt_block(j):
        lo = j * NT_BLOCK
        store_chunks(qt_ref, lo, proj_t(OFF_Q + lo, NT_BLOCK) * (M_DQK ** -0.5))

    def vt_block(j):
        lo = j * NT_BLOCK
        store_chunks(vt_ref, lo, proj_t(OFF_V + lo, NT_BLOCK))

    def ogt_block(j):
        lo = j * NT_BLOCK
        gate = _sigmoid(proj_t(OFF_O + lo, NT_BLOCK)) * _silu(proj_t(OFF_ZM + lo, NT_BLOCK))
        store_chunks(ogt_ref, lo, gate * mnw_ref[lo:lo + NT_BLOCK, :])

    def zst_block(j):
        lo = j * NT_BLOCK
        store_chunks(zst_ref, lo, _silu(proj_t(OFF_ZS + lo, NT_BLOCK)))

    def gate_block(_):
        g = (lax.dot_general(wg_ref[...], xn_ref[...], NT, preferred_element_type=F32)
             + gb_ref[...])
        row = lax.broadcasted_iota(jnp.int32, g.shape, 0)
        capped = GATE_CAP * jnp.tanh(g / GATE_CAP)
        log_f = jnp.minimum(capped, 0.0) - jnp.log1p(jnp.exp(-jnp.abs(capped)))
        gates = jnp.where(row < G_F, capped,
                          jnp.where(row < G_DT, log_f,
                                    jnp.where(row < G_END, _softplus(g), 0.0)))
        store_chunks(gt_ref, 0, gates)

    blocks = {"x": xbc_block, "k": k_block, "q": qt_block, "v": vt_block, "o": ogt_block,
              "z": zst_block, "g": gate_block}
    for name in _IN_PROJ_SCHEDULE:
        blocks[name[0]](int(name[1:]))


def _in_proj(x2, nw, w_n, w_t, w_g, gate_bias, conv_w_t, conv_b, mnw_col, *, bsz, seq):
    tm = TM_IN
    tiles_per_seq = seq // tm
    n_chunks = seq // CHUNK
    cpt = tm // CHUNK
    grid = (bsz * tiles_per_seq,)
    row = lambda i: (i, 0)
    fixed = lambda i: (0, 0)
    chunked = lambda i: (i // tiles_per_seq, i % tiles_per_seq, 0, 0)

    def rowmajor(width):
        return (jax.ShapeDtypeStruct((bsz * seq, width), BF16), pl.BlockSpec((tm, width), row))

    def transposed(channels, dtype):
        return (jax.ShapeDtypeStruct((bsz, n_chunks, channels, CHUNK), dtype),
                pl.BlockSpec((None, cpt, channels, CHUNK), chunked))

    outs = [transposed(M_QK, BF16), rowmajor(M_QK),
            transposed(M_WIDTH, BF16), transposed(M_WIDTH, BF16),
            transposed(S_WIDTH, BF16), transposed(S_WIDTH, BF16),
            rowmajor(S_GROUPS * S_STATE), rowmajor(S_GROUPS * S_STATE),
            transposed(G_ROWS, F32)]
    return pl.pallas_call(
        functools.partial(_in_proj_kernel, tiles_per_seq=tiles_per_seq),
        grid=grid,
        in_specs=[
            pl.BlockSpec((tm, D_MODEL), row),
            pl.BlockSpec((1, D_MODEL), fixed),
            pl.BlockSpec((D_MODEL, N_ROWMAJOR), fixed),
            pl.BlockSpec((N_TRANSPOSED, D_MODEL), fixed),
            pl.BlockSpec((G_ROWS, D_MODEL), fixed),
            pl.BlockSpec((G_ROWS, 1), fixed),
            pl.BlockSpec((S_CONV, S_CONV_CH), fixed),
            pl.BlockSpec((1, S_CONV_CH), fixed),
            pl.BlockSpec((M_WIDTH, 1), fixed),
        ],
        out_specs=tuple(o[1] for o in outs),
        out_shape=tuple(o[0] for o in outs),
        scratch_shapes=[
            pltpu.VMEM((tm, D_MODEL), BF16),
            pltpu.VMEM((tm + SUBLANES, S_CONV_CH), F32),
        ],
        compiler_params=pltpu.CompilerParams(
            dimension_semantics=("arbitrary",),
            vmem_limit_bytes=VMEM_LIMIT_BYTES),
        name="in_proj",
    )(x2, nw, w_n, w_t, w_g, gate_bias, conv_w_t, conv_b, mnw_col)


def _cumsum_lanes(rows, upper):
    return jnp.dot(rows, upper, preferred_element_type=F32, precision=lax.Precision.HIGHEST)


def _column_of(row):
    return jnp.broadcast_to(row, (CHUNK, CHUNK)).T


def _mlstm_head(h, c, b_all, u_all, causal, qt_ref, k_ref, vt_ref, ogt_ref, o_ref, ct_ref, m_ref):
    rows = slice(c * CHUNK, (c + 1) * CHUNK)
    ch = slice(h * M_DV, (h + 1) * M_DV)
    qt = qt_ref[c, h * M_DQK:(h + 1) * M_DQK, :]
    k = k_ref[rows, h * M_DQK:(h + 1) * M_DQK]
    ones = jnp.ones((BF16_ROWS, CHUNK), BF16)
    vt_aug = jnp.concatenate([vt_ref[c, ch, :], ones], axis=0)
    b = b_all[h:h + 1, :]
    u_col = _column_of(u_all[h:h + 1, :])
    u_msk = jnp.where(causal, u_col, -jnp.inf)
    u_max = jnp.max(u_msk, axis=0, keepdims=True)
    qk = jnp.dot(k, qt, preferred_element_type=F32)
    yield

    m_prev = m_ref[h][:, 0:1]
    ct_prev = ct_ref[h]
    mm = jnp.maximum(m_prev, u_max)
    st = qk * jnp.exp2(u_msk - mm)
    inter_scale = jnp.exp2(m_prev - mm)
    lhs = jnp.concatenate([vt_aug, ct_prev.astype(BF16)], axis=1)
    rhs = jnp.concatenate([st.astype(BF16), (qt.astype(F32) * inter_scale).astype(BF16)], axis=0)
    num_aug = jnp.dot(lhs, rhs, preferred_element_type=F32)
    mm_last = mm[:, CHUNK - 1:CHUNK]
    kw = (k.astype(F32) * jnp.exp2(u_col - mm_last)).astype(BF16)
    ct_ref[h] = (inter_scale[:, CHUNK - 1:CHUNK] * ct_prev
                 + jnp.dot(vt_aug, kw, preferred_element_type=F32))
    m_ref[h] = jnp.broadcast_to(b[:, CHUNK - 1:CHUNK] + mm_last, (1, LANES))
    yield

    num = num_aug[0:M_DV, :]
    den = num_aug[M_DV:M_DV + 1, :]
    rden = 1.0 / jnp.maximum(jnp.abs(den), jnp.exp2(-(b + mm)))
    msq = jnp.mean(num * num, axis=0, keepdims=True)
    scale = rden * lax.rsqrt(rden * rden * msq + EPS)
    o_ref[c, ch, :] = (num * scale).astype(BF16) * ogt_ref[c, ch, :]


def _ssd_group(g, c, dt, acum, ea, decay_end, chunk_decay, causal, xst_ref, bm_ref, cm_ref,
               zst_ref, dskip_ref, nw_ref, o_ref, h_ref):
    rows = slice(c * CHUNK, (c + 1) * CHUNK)
    bm = bm_ref[rows, g * S_STATE:(g + 1) * S_STATE]
    cm = cm_ref[rows, g * S_STATE:(g + 1) * S_STATE]
    cbt = lax.dot_general(bm, cm, NT, preferred_element_type=F32)
    yield
    yds, xds = [], []
    for r in range(S_HEADS_PER_GROUP):
        i = g * S_HEADS_PER_GROUP + r
        ch = slice(i * S_HEADDIM, (i + 1) * S_HEADDIM)
        a_row = acum[i:i + 1, :]
        decay = jnp.exp2(jnp.where(causal, a_row - _column_of(a_row), -jnp.inf))
        mt = (cbt * decay).astype(BF16)
        xdt = xst_ref[c, ch, :].astype(F32) * dt[i:i + 1, :]
        yds.append(jnp.dot(xdt.astype(BF16), mt, preferred_element_type=F32))
        xds.append((xdt * decay_end[i:i + 1, :]).astype(BF16))
        yield

    h_prev = h_ref[g]
    y_off = lax.dot_general(h_prev.astype(BF16), cm, NT, preferred_element_type=F32)
    upd = jnp.dot(jnp.concatenate(xds, axis=0), bm, preferred_element_type=F32)
    for r in range(S_HEADS_PER_GROUP):
        i = g * S_HEADS_PER_GROUP + r
        hs = slice(r * S_HEADDIM, (r + 1) * S_HEADDIM)
        h_ref[g, hs, :] = chunk_decay[i:i + 1, :] * h_prev[hs, :] + upd[hs, :]
    yield

    ys = []
    for r in range(S_HEADS_PER_GROUP):
        i = g * S_HEADS_PER_GROUP + r
        ch = slice(i * S_HEADDIM, (i + 1) * S_HEADDIM)
        y = (yds[r] + y_off[r * S_HEADDIM:(r + 1) * S_HEADDIM, :] * ea[i:i + 1, :]
             + dskip_ref[ch, :] * xst_ref[c, ch, :].astype(F32))
        ys.append(y * zst_ref[c, ch, :].astype(F32))
    gch = slice(g * S_GROUP_WIDTH, (g + 1) * S_GROUP_WIDTH)
    yg = jnp.concatenate(ys, axis=0)
    yg = yg * lax.rsqrt(jnp.mean(yg * yg, axis=0, keepdims=True) + EPS) * nw_ref[gch, :]
    och = slice(M_WIDTH + g * S_GROUP_WIDTH, M_WIDTH + (g + 1) * S_GROUP_WIDTH)
    o_ref[c, och, :] = yg.astype(o_ref.dtype)


def _interleave(a, b):
    if len(a) < len(b):
        a, b = b, a
    out, taken = [], 0
    for n, item in enumerate(a):
        out.append(item)
        want = min(len(b), (2 * (n + 1) * len(b) + len(a)) // (2 * len(a)))
        out.extend(b[taken:want])
        taken = want
    return out


def _mixer_out_kernel(qt_ref, k_ref, vt_ref, ogt_ref, zst_ref, xst_ref, bm_ref, cm_ref, gt_ref,
                      snw_ref, dskip_ref, alog_ref, x_ref, wo_ref, onw_ref,
                      o_ref, ct_ref, m_ref, h_ref, mix_ref, mixrm_ref, acc_ref):
    step = pl.program_id(1)
    slot = lax.rem(step, 2)

    @pl.when(step == 0)
    def _():
        ct_ref[...] = jnp.zeros(ct_ref.shape, F32)
        m_ref[...] = jnp.zeros(m_ref.shape, F32)
        h_ref[...] = jnp.zeros(h_ref.shape, F32)
        mix_ref[...] = jnp.zeros(mix_ref.shape, BF16)

    cur_ref = mix_ref.at[slot]
    prev_ref = mix_ref.at[1 - slot]
    n = vt_ref.shape[0]

    def out_prep():
        for c in range(n):
            mixrm_ref[c * CHUNK:(c + 1) * CHUNK, :] = prev_ref[c].T
        yield

    def out_block(j):
        ch = slice(j * OUT_BLOCK, (j + 1) * OUT_BLOCK)
        acc_ref[:, ch] = x_ref[:, ch] + jnp.dot(mixrm_ref[...], wo_ref[:, ch],
                                                preferred_element_type=F32)
        yield

    def out_final():
        acc = acc_ref[...]
        ms = jnp.mean(acc * acc, axis=-1, keepdims=True)
        o_ref[...] = acc * lax.rsqrt(ms + EPS) * onw_ref[...]
        yield

    r = lax.broadcasted_iota(jnp.int32, (CHUNK, CHUNK), 0)
    l = lax.broadcasted_iota(jnp.int32, (CHUNK, CHUNK), 1)
    causal = r <= l
    upper = jnp.where(causal, 1.0, 0.0).astype(F32)
    a_neg = -LOG2E * jnp.exp(alog_ref[...])

    def chunk_tasks(c):
        gates = gt_ref[c]
        cum = _cumsum_lanes(gates[0:SUBLANES, :] * LOG2E, upper)
        b_all = cum[G_F:G_F + M_HEADS, :]
        u_all = gates[G_I:G_I + M_HEADS, :] * LOG2E - b_all
        dt = gates[G_DT:G_END, :]
        acum = _cumsum_lanes(dt * a_neg, upper)
        a_last = acum[:, CHUNK - 1:CHUNK]
        ea, decay_end, chunk_decay = jnp.exp2(acum), jnp.exp2(a_last - acum), jnp.exp2(a_last)
        ml = [_mlstm_head(h, c, b_all, u_all, causal, qt_ref, k_ref, vt_ref, ogt_ref,
                          cur_ref, ct_ref, m_ref) for h in range(M_HEADS)]
        sg = [_ssd_group(g, c, dt, acum, ea, decay_end, chunk_decay, causal, xst_ref, bm_ref,
                         cm_ref, zst_ref, dskip_ref, snw_ref, cur_ref, h_ref)
              for g in range(S_GROUPS)]
        n_s = 1 + S_HEADS_PER_GROUP
        early = _interleave([s for _ in range(n_s) for s in sg], ml)
        late = _interleave(sg + ml + sg, ml)
        return early, late

    n_out = D_MODEL // OUT_BLOCK
    out_tasks = [out_prep()] + [out_block(j) for j in range(n_out)] + [out_final()]
    prev_late = []
    for c in range(n + 1):
        early, late = chunk_tasks(c) if c < n else ([], [])
        lo, hi = (c * len(out_tasks)) // (n + 1), ((c + 1) * len(out_tasks)) // (n + 1)
        for t in _interleave(_interleave(early, prev_late), out_tasks[lo:hi]):
            next(t, None)
        prev_late = late


def _mixer_out(qt, k, vt, ogt, zst, xst, bm, cm, gt, snw_x, dskip_x, alog_x, x, w_o, onw):
    bsz, n_chunks = vt.shape[0], vt.shape[1]
    cps = CHUNKS_PER_STEP
    tm = cps * CHUNK
    n_steps = n_chunks // cps
    grid = (bsz, n_steps + 1)
    cur = lambda s: jnp.minimum(s, n_steps - 1)
    lag = lambda s: jnp.maximum(s - 1, 0)
    rowmajor = lambda w: pl.BlockSpec((None, tm, w), lambda b, s: (b, cur(s), 0))
    transposed = lambda ch: pl.BlockSpec((None, cps, ch, CHUNK), lambda b, s: (b, cur(s), 0, 0))
    fixed = lambda a: pl.BlockSpec(a.shape, lambda b, s: (0, 0))
    lagged = pl.BlockSpec((None, tm, D_MODEL), lambda b, s: (b, lag(s), 0))
    return pl.pallas_call(
        _mixer_out_kernel,
        grid=grid,
        in_specs=[transposed(M_QK), rowmajor(M_QK), transposed(M_WIDTH), transposed(M_WIDTH),
                  transposed(S_WIDTH), transposed(S_WIDTH),
                  rowmajor(S_GROUPS * S_STATE), rowmajor(S_GROUPS * S_STATE),
                  transposed(G_ROWS),
                  fixed(snw_x), fixed(dskip_x), fixed(alog_x), lagged, fixed(w_o), fixed(onw)],
        out_specs=lagged,
        out_shape=jax.ShapeDtypeStruct(x.shape, F32),
        scratch_shapes=[
            pltpu.VMEM((M_HEADS, M_DV + BF16_ROWS, M_DQK), F32),
            pltpu.VMEM((M_HEADS, 1, LANES), F32),
            pltpu.VMEM((S_GROUPS, S_GROUP_WIDTH, S_STATE), F32),
            pltpu.VMEM((2, cps, D_INNER, CHUNK), BF16),
            pltpu.VMEM((tm, D_INNER), BF16),
            pltpu.VMEM((tm, D_MODEL), F32),
        ],
        compiler_params=pltpu.CompilerParams(
            dimension_semantics=("arbitrary", "arbitrary"),
            vmem_limit_bytes=VMEM_LIMIT_BYTES),
        name="mixer_out",
    )(qt, k, vt, ogt, zst, xst, bm, cm, gt, snw_x, dskip_x, alog_x, x, w_o, onw)


def _lane_expand(v):
    return jnp.broadcast_to(v.astype(F32).reshape(-1, 1), (v.size, LANES))


def _layer(x, norm_w, w_in, b_igate, b_fgate, conv_w, conv_b, dt_bias, a_log, d_skip,
           mlstm_norm_w, ssd_norm_w, w_out, out_norm_w):
    bsz, seq, _ = x.shape
    assert seq % max(TM_IN, CHUNK * CHUNKS_PER_STEP) == 0
    n_tok = bsz * seq
    x2 = x.reshape(n_tok, D_MODEL)

    edges = [0]
    for w in (M_QK, M_QK, M_WIDTH, M_HEADS, M_HEADS, M_WIDTH, M_WIDTH, S_WIDTH, S_CONV_CH, S_HEADS):
        edges.append(edges[-1] + w)
    w_bf = w_in.astype(BF16)
    seg = lambda j: w_bf[:, edges[j]:edges[j + 1]]
    w_n = jnp.concatenate([seg(1), seg(8)], axis=1)
    w_t = jnp.concatenate([seg(0), seg(2), seg(5), seg(6), seg(7)], axis=1).T
    w_g = jnp.concatenate([seg(3), seg(4), seg(9)], axis=1).T
    w_g = jnp.pad(w_g, ((0, G_ROWS - w_g.shape[0]), (0, 0)))
    gate_bias = jnp.concatenate([b_igate, b_fgate, dt_bias]).astype(F32)
    gate_bias = jnp.pad(gate_bias, (0, G_ROWS - gate_bias.shape[0])).reshape(G_ROWS, 1)

    qt, k, vt, ogt, zst, xst, bm, cm, gt = _in_proj(
        x2, norm_w.reshape(1, D_MODEL), w_n, w_t, w_g, gate_bias,
        conv_w.T.astype(F32), conv_b.reshape(1, S_CONV_CH).astype(F32),
        mlstm_norm_w.astype(F32).reshape(M_WIDTH, 1), bsz=bsz, seq=seq)

    r3 = lambda a: a.reshape(bsz, seq, a.shape[-1])
    return _mixer_out(qt, r3(k), vt, ogt, zst, xst, r3(bm), r3(cm), gt,
                      _lane_expand(ssd_norm_w), _lane_expand(jnp.repeat(d_skip, S_HEADDIM)),
                      jnp.broadcast_to(a_log.astype(F32).reshape(S_HEADS, 1), (S_HEADS, CHUNK)),
                      x, w_out.astype(BF16), out_norm_w.reshape(1, D_MODEL))


def kernel(x, norm_w, w_in, b_igate, b_fgate, conv_w, conv_b, dt_bias, a_log, d_skip,
           mlstm_norm_w, ssd_norm_w, w_out, final_norm_w):
    depth = norm_w.shape[0]
    assert depth == 1, "final RMSNorm is fused into the single layer's output projection"
    return _layer(x, norm_w[0], w_in[0], b_igate[0], b_fgate[0], conv_w[0], conv_b[0],
                  dt_bias[0], a_log[0], d_skip[0], mlstm_norm_w[0], ssd_norm_w[0], w_out[0],
                  final_norm_w)
```
